```python
import jax, jax.numpy as jnp
from jax import lax
import numpy as np

D_MODEL = 1024
BATCH = 1
SEQ = 16384
DEPTH = 4
DEC_BATCH = 16
DEC_SEQ = 4096
PAST_LEN = 128

RET_HEADS = 8
RET_DK = 64
RET_DV = 128
D_QK = RET_HEADS * RET_DK
D_V = RET_HEADS * RET_DV
CHUNK = 128
ROPE_BASE = 10000.0
POOL_WINDOWS = (2, 4, 8, 16)
POOL_GROUPS = 4
D_POOL = D_MODEL
POOL_GW = D_POOL // POOL_GROUPS
D_FF = 4 * D_MODEL
EPS = 1e-6
IN_SPLITS = (D_QK, D_QK, D_V, D_V, D_POOL, D_MODEL, D_MODEL)
D_IN = D_QK + D_QK + D_V + D_V + D_POOL + D_MODEL + D_MODEL

kernel_name = "hybrid_retention_pool_encoder"


def rms_norm(x, g):
    xf = x.astype(jnp.float32)
    y = xf * lax.rsqrt(jnp.mean(xf * xf, axis=-1, keepdims=True) + EPS)
    return (y * g.astype(jnp.float32)).astype(x.dtype)


def rope(x):
    s, d = x.shape[1], x.shape[-1]
    half = d // 2
    inv = ROPE_BASE ** (-jnp.arange(half, dtype=jnp.float32) / half)
    ang = jnp.arange(s, dtype=jnp.float32)[:, None] * inv[None, :]
    cos = jnp.cos(ang)[None, :, None, :]
    sin = jnp.sin(ang)[None, :, None, :]
    xf = x.astype(jnp.float32)
    x1, x2 = xf[..., :half], xf[..., half:]
    return jnp.concatenate([x1 * cos - x2 * sin, x1 * sin + x2 * cos], axis=-1)


def retention_scan(q, k, v, log_g, strict):
    b, s, h, dk = q.shape
    dv = v.shape[-1]
    n = s // CHUNK
    qc = q.reshape(b, n, CHUNK, h, dk)
    kc = k.reshape(b, n, CHUNK, h, dk)
    vc = v.reshape(b, n, CHUNK, h, dv)
    idx = jnp.arange(CHUNK, dtype=jnp.float32)
    diff = idx[:, None] - idx[None, :]
    mask = diff > 0 if strict else diff >= 0
    safe = jnp.where(mask, diff, 0.0)
    decay = jnp.where(mask[None], jnp.exp(safe[None] * log_g[:, None, None]), 0.0)
    scores = jnp.einsum('bnihd,bnjhd->bnhij', qc, kc) * decay[None, None]
    intra = jnp.einsum('bnhij,bnjhe->bnihe', scores, vc)
    zeta = jnp.exp((CHUNK - 1.0 - idx)[None, :] * log_g[:, None])
    xi = jnp.exp((idx + 1.0)[None, :] * log_g[:, None])
    chunk_decay = jnp.exp(CHUNK * log_g)[None, :, None, None]
    kv = jnp.einsum('bnjhd,hj,bnjhe->nbhde', kc, zeta, vc)

    def step(state, kv_n):
        return chunk_decay * state + kv_n, state

    _, prev_states = lax.scan(step, jnp.zeros((b, h, dk, dv), jnp.float32), kv)
    inter = jnp.einsum('bnihd,hi,nbhde->bnihe', qc, xi, prev_states)
    return (intra + inter).reshape(b, s, h, dv)


def retention_branch(q, k, v, g, decay_fwd, decay_bwd, gn_gain):
    b, s, _ = q.shape
    qh = rope(q.reshape(b, s, RET_HEADS, RET_DK))
    kh = rope(k.reshape(b, s, RET_HEADS, RET_DK)) * (RET_DK ** -0.5)
    vh = v.reshape(b, s, RET_HEADS, RET_DV).astype(jnp.float32)
    lg_f = jax.nn.log_sigmoid(decay_fwd.astype(jnp.float32))
    lg_b = jax.nn.log_sigmoid(decay_bwd.astype(jnp.float32))
    o_f = retention_scan(qh, kh, vh, lg_f, strict=False)
    o_b = jnp.flip(retention_scan(jnp.flip(qh, 1), jnp.flip(kh, 1), jnp.flip(vh, 1), lg_b, strict=True), 1)
    o = o_f + o_b
    mu = jnp.mean(o, axis=-1, keepdims=True)
    var = jnp.mean(jnp.square(o - mu), axis=-1, keepdims=True)
    o = (o - mu) * lax.rsqrt(var + EPS) * gn_gain.astype(jnp.float32).reshape(RET_HEADS, RET_DV)
    y = jax.nn.silu(g.astype(jnp.float32)) * o.reshape(b, s, D_V)
    return y.astype(g.dtype)


def pool_branch(u, w_pool, scale):
    b, s, _ = u.shape
    uf = u.astype(jnp.float32).reshape(b, s, POOL_GROUPS, POOL_GW)
    cs = jnp.concatenate([jnp.zeros((b, 1, POOL_GROUPS, POOL_GW), jnp.float32), jnp.cumsum(uf, axis=1)], axis=1)
    pos = jnp.arange(s)
    outs = []
    for gi, w in enumerate(POOL_WINDOWS):
        half = w // 2
        hi = jnp.minimum(pos + half, s)
        lo = jnp.maximum(pos - half, 0)
        csg = cs[:, :, gi]
        seg = jnp.take(csg, hi, axis=1) - jnp.take(csg, lo, axis=1)
        mean = seg / (hi - lo).astype(jnp.float32)[None, :, None]
        outs.append(mean - uf[:, :, gi])
    p = jnp.stack(outs, axis=2)
    y = jnp.einsum('bsgc,gcd->bsgd', p, w_pool.astype(jnp.float32)).reshape(b, s, D_POOL)
    return (y * scale.astype(jnp.float32)).astype(u.dtype)


def trunk(x, norm_mix_pre, norm_mix_post, w_in, ret_decay_fwd, ret_decay_bwd, ret_gn,
          pool_w, pool_scale, w_out, norm_mlp_pre, norm_mlp_post, w_mlp1, w_mlp2):
    cuts = [int(c) for c in np.cumsum(IN_SPLITS)[:-1]]
    for l in range(DEPTH):
        h = rms_norm(x, norm_mix_pre[l])
        proj = jnp.einsum('bsd,de->bse', h, w_in[l])
        q, k, v, g, u, gr, gp = jnp.split(proj, cuts, axis=-1)
        y_r = retention_branch(q, k, v, g, ret_decay_fwd[l], ret_decay_bwd[l], ret_gn[l])
        y_p = pool_branch(u, pool_w[l], pool_scale[l])
        m = jax.nn.sigmoid(gr) * y_r + jax.nn.sigmoid(gp) * y_p
        x = x + rms_norm(jnp.einsum('bsd,de->bse', m, w_out[l]), norm_mix_post[l])
        h2 = rms_norm(x, norm_mlp_pre[l])
        f = jnp.square(jax.nn.relu(jnp.einsum('bsd,df->bsf', h2, w_mlp1[l])))
        f = jnp.einsum('bsf,fd->bsd', f, w_mlp2[l])
        x = x + rms_norm(f, norm_mlp_post[l])
    return x


def setup_inputs(seed: int = 0) -> dict:
    key = jax.random.key(seed)
    ks = jax.random.split(key, 16)
    f32 = jnp.float32
    base_decay = jnp.log(2.0 ** (5.0 + jnp.arange(RET_HEADS, dtype=f32)) - 1.0)
    return {
        "x_prompt": jax.random.normal(ks[0], (BATCH, SEQ, D_MODEL), f32),
        "x_sample": jax.random.normal(ks[1], (DEC_BATCH, DEC_SEQ, D_MODEL), f32),
        "norm_mix_pre": 1.0 + 0.05 * jax.random.normal(ks[2], (DEPTH, D_MODEL), f32),
        "norm_mix_post": 1.0 + 0.05 * jax.random.normal(ks[3], (DEPTH, D_MODEL), f32),
        "w_in": jax.random.normal(ks[4], (DEPTH, D_MODEL, D_IN), f32) * D_MODEL ** -0.5,
        "ret_decay_fwd": base_decay[None, :] + 0.1 * jax.random.normal(ks[5], (DEPTH, RET_HEADS), f32),
        "ret_decay_bwd": base_decay[None, :] + 0.1 * jax.random.normal(ks[6], (DEPTH, RET_HEADS), f32),
        "ret_gn": 1.0 + 0.05 * jax.random.normal(ks[7], (DEPTH, D_V), f32),
        "pool_w": jax.random.normal(ks[8], (DEPTH, POOL_GROUPS, POOL_GW, POOL_GW), f32) * POOL_GW ** -0.5,
        "pool_scale": 1.0 + 0.05 * jax.random.normal(ks[9], (DEPTH, D_POOL), f32),
        "w_out": jax.random.normal(ks[10], (DEPTH, D_MODEL, D_MODEL), f32) * D_MODEL ** -0.5,
        "norm_mlp_pre": 1.0 + 0.05 * jax.random.normal(ks[11], (DEPTH, D_MODEL), f32),
        "norm_mlp_post": 1.0 + 0.05 * jax.random.normal(ks[12], (DEPTH, D_MODEL), f32),
        "w_mlp1": jax.random.normal(ks[13], (DEPTH, D_MODEL, D_FF), f32) * D_MODEL ** -0.5,
        "w_mlp2": jax.random.normal(ks[14], (DEPTH, D_FF, D_MODEL), f32) * D_FF ** -0.5,
    }


def reference(x_prompt, x_sample, norm_mix_pre, norm_mix_post, w_in, ret_decay_fwd, ret_decay_bwd,
              ret_gn, pool_w, pool_scale, w_out, norm_mlp_pre, norm_mlp_post, w_mlp1, w_mlp2):
    y_prompt = trunk(x_prompt, norm_mix_pre, norm_mix_post, w_in, ret_decay_fwd, ret_decay_bwd, ret_gn,
                     pool_w, pool_scale, w_out, norm_mlp_pre, norm_mlp_post, w_mlp1, w_mlp2)
    y_sample = trunk(x_sample, norm_mix_pre, norm_mix_post, w_in, ret_decay_fwd, ret_decay_bwd, ret_gn,
                     pool_w, pool_scale, w_out, norm_mlp_pre, norm_mlp_post, w_mlp1, w_mlp2)
    return (y_prompt, y_sample)
```

```python
import functools

import numpy as np
import jax
import jax.numpy as jnp
from jax import lax
from jax.experimental import pallas as pl
from jax.experimental.pallas import tpu as pltpu

D_MODEL = 1024
DEPTH = 4
HEADS = 8
DK = 64
DV = 128
PAIRS = HEADS // 2
D_QK = HEADS * DK
CHUNK = 128
ROPE_BASE = 10000.0
POOL_WINDOWS = (2, 4, 8, 16)
POOL_GW = 256
D_FF = 4 * D_MODEL
EPS = 1e-6
QK_SCALE = DK ** -0.5

OFF_Q, OFF_K, OFF_QF, OFF_QB = 0, 512, 1024, 1536
OFF_V, OFF_G, OFF_U, OFF_GR, OFF_GP = 2048, 3072, 4096, 5120, 6144
D_PROJ = 7168
HALO = 16

TB_IN = 512
TB_MIX = 512
TB_MLP = 512
VMEM_LIMIT_BYTES = 56 * 1024 * 1024

F32 = jnp.float32
BF16 = jnp.bfloat16


def _rms(x, g):
    return x * lax.rsqrt(jnp.mean(x * x, axis=-1, keepdims=True) + EPS) * g


def _log_sigmoid(d):
    return jnp.minimum(d, 0.0) - jnp.log1p(jnp.exp(-jnp.abs(d)))


def _params(n_axes=1):
    return pltpu.CompilerParams(dimension_semantics=("arbitrary",) * n_axes,
                                vmem_limit_bytes=VMEM_LIMIT_BYTES)


def _tables_kernel(dec_ref, m_ref, xif_ref, xib_ref, zft_ref, zbt_ref, dcf_ref, dcb_ref):
    l = pl.program_id(0)
    c = CHUNK
    df = [dec_ref[l, h] for h in range(HEADS)]
    db = [dec_ref[l, HEADS + h] for h in range(HEADS)]

    ii = lax.broadcasted_iota(jnp.int32, (c, c), 0)
    jj = lax.broadcasted_iota(jnp.int32, (c, c), 1)
    dist = jnp.abs(ii - jj).astype(F32)
    for h in range(HEADS):
        dsel = jnp.where(ii >= jj, jnp.full((c, c), df[h], F32), jnp.full((c, c), db[h], F32))
        m_ref[0, h] = jnp.exp(dist * _log_sigmoid(dsel))

    rows = xif_ref.shape[1]
    lane = lax.broadcasted_iota(jnp.int32, (rows, D_QK), 1)
    head = 2 * (lane >> 7) + ((lane >> 5) & 1)
    idx = (lax.broadcasted_iota(jnp.int32, (rows, D_QK), 0) & (c - 1)).astype(F32)
    sel_f = jnp.full((rows, D_QK), df[0], F32)
    sel_b = jnp.full((rows, D_QK), db[0], F32)
    for h in range(1, HEADS):
        sel_f = jnp.where(head == h, df[h], sel_f)
        sel_b = jnp.where(head == h, db[h], sel_b)
    xif_ref[0] = jnp.exp((idx + 1.0) * _log_sigmoid(sel_f))
    xib_ref[0] = jnp.exp((c - idx) * _log_sigmoid(sel_b))

    rl = lax.broadcasted_iota(jnp.int32, (2 * DK, c), 0)
    odd = ((rl >> 5) & 1) == 1
    jc = lax.broadcasted_iota(jnp.int32, (2 * DK, c), 1).astype(F32)
    for p in range(PAIRS):
        lg_f = _log_sigmoid(jnp.where(odd, df[2 * p + 1], df[2 * p]))
        lg_b = _log_sigmoid(jnp.where(odd, db[2 * p + 1], db[2 * p]))
        zft_ref[0, p] = jnp.exp((c - 1.0 - jc) * lg_f)
        zbt_ref[0, p] = jnp.exp(jc * lg_b)
        dcf_ref[0, p] = jnp.exp(float(c) * lg_f)
        dcb_ref[0, p] = jnp.exp(float(c) * lg_b)


def _decay_tables(dec):
    c = CHUNK
    out_shape = (
        jax.ShapeDtypeStruct((DEPTH, HEADS, c, c), F32),
        jax.ShapeDtypeStruct((DEPTH, TB_IN, D_QK), F32),
        jax.ShapeDtypeStruct((DEPTH, TB_IN, D_QK), F32),
        jax.ShapeDtypeStruct((DEPTH, PAIRS, 2 * DK, c), F32),
        jax.ShapeDtypeStruct((DEPTH, PAIRS, 2 * DK, c), F32),
        jax.ShapeDtypeStruct((DEPTH, PAIRS, 2 * DK, c), F32),
        jax.ShapeDtypeStruct((DEPTH, PAIRS, 2 * DK, c), F32),
    )
    out_specs = (
        pl.BlockSpec((1, HEADS, c, c), lambda l: (l, 0, 0, 0)),
        pl.BlockSpec((1, TB_IN, D_QK), lambda l: (l, 0, 0)),
        pl.BlockSpec((1, TB_IN, D_QK), lambda l: (l, 0, 0)),
        pl.BlockSpec((1, PAIRS, 2 * DK, c), lambda l: (l, 0, 0, 0)),
        pl.BlockSpec((1, PAIRS, 2 * DK, c), lambda l: (l, 0, 0, 0)),
        pl.BlockSpec((1, PAIRS, 2 * DK, c), lambda l: (l, 0, 0, 0)),
        pl.BlockSpec((1, PAIRS, 2 * DK, c), lambda l: (l, 0, 0, 0)),
    )
    return pl.pallas_call(
        _tables_kernel,
        grid=(DEPTH,),
        in_specs=[pl.BlockSpec(memory_space=pltpu.SMEM)],
        out_specs=out_specs,
        out_shape=out_shape,
        compiler_params=_params(),
    )(dec)


def _inproj_kernel(x_ref, g_ref, w_ref, cos_ref, sin_ref, xif_ref, xib_ref, zft_ref, zbt_ref,
                   proj_ref, kvf_ref, kvb_ref, k_scr):
    tb = x_ref.shape[0]
    h = _rms(x_ref[...], g_ref[...]).astype(BF16)
    cos = cos_ref[...]
    sin = sin_ref[...]

    qk = jnp.dot(h, w_ref[:, 0:2 * D_QK], preferred_element_type=F32)
    for p in range(PAIRS):
        lo = p * 128
        qp = qk[:, lo:lo + 128]
        qr = (qp * cos + pltpu.roll(qp, 64, 1) * sin) * QK_SCALE
        proj_ref[:, OFF_Q + lo:OFF_Q + lo + 128] = qr.astype(BF16)
        proj_ref[:, OFF_QF + lo:OFF_QF + lo + 128] = (qr * xif_ref[:, lo:lo + 128]).astype(BF16)
        proj_ref[:, OFF_QB + lo:OFF_QB + lo + 128] = (qr * xib_ref[:, lo:lo + 128]).astype(BF16)
        kp = qk[:, D_QK + lo:D_QK + lo + 128]
        kr = kp * cos + pltpu.roll(kp, 64, 1) * sin
        proj_ref[:, OFF_K + lo:OFF_K + lo + 128] = kr.astype(BF16)
        k_scr[:, lo:lo + 128] = kr

    for s in range(5):
        lo = 2 * D_QK + s * D_MODEL
        y = jnp.dot(h, w_ref[:, lo:lo + D_MODEL], preferred_element_type=F32)
        proj_ref[:, OFF_V + s * D_MODEL:OFF_V + (s + 1) * D_MODEL] = y.astype(BF16)

    rl = lax.broadcasted_iota(jnp.int32, (2 * DK, DV), 0)
    even_row = ((rl >> 5) & 1) == 0
    for c in range(tb // CHUNK):
        r0 = c * CHUNK
        for p in range(PAIRS):
            kt = k_scr[r0:r0 + CHUNK, p * 128:(p + 1) * 128].T
            vp = proj_ref[r0:r0 + CHUNK, OFF_V + 2 * p * DV:OFF_V + (2 * p + 2) * DV]
            for zt_ref, kv_ref in ((zft_ref, kvf_ref), (zbt_ref, kvb_ref)):
                a = (kt * zt_ref[p]).astype(BF16)
                kv = jnp.dot(a, vp, preferred_element_type=F32)
                kv_ref[c, p] = jnp.where(even_row, kv[:, 0:DV], kv[:, DV:2 * DV])


def _inproj(x, g, w, cos_t, sin_t, xif, xib, zft, zbt, seq_len):
    t = x.shape[0]
    tb = TB_IN
    nb_seq = seq_len // tb
    cpb = tb // CHUNK
    const2 = lambda i: (0, 0)
    const3 = lambda i: (0, 0, 0)
    return pl.pallas_call(
        _inproj_kernel,
        grid=(t // tb,),
        in_specs=[
            pl.BlockSpec((tb, D_MODEL), lambda i: (i, 0)),
            pl.BlockSpec((1, D_MODEL), const2),
            pl.BlockSpec(w.shape, const2),
            pl.BlockSpec((tb, 128), lambda i: (i % nb_seq, 0)),
            pl.BlockSpec((tb, 128), lambda i: (i % nb_seq, 0)),
            pl.BlockSpec((tb, D_QK), const2),
            pl.BlockSpec((tb, D_QK), const2),
            pl.BlockSpec((PAIRS, 2 * DK, CHUNK), const3),
            pl.BlockSpec((PAIRS, 2 * DK, CHUNK), const3),
        ],
        out_specs=(
            pl.BlockSpec((tb, D_PROJ), lambda i: (i, 0)),
            pl.BlockSpec((cpb, PAIRS, 2 * DK, DV), lambda i: (i, 0, 0, 0)),
            pl.BlockSpec((cpb, PAIRS, 2 * DK, DV), lambda i: (i, 0, 0, 0)),
        ),
        out_shape=(
            jax.ShapeDtypeStruct((t, D_PROJ), BF16),
            jax.ShapeDtypeStruct((t // CHUNK, PAIRS, 2 * DK, DV), F32),
            jax.ShapeDtypeStruct((t // CHUNK, PAIRS, 2 * DK, DV), F32),
        ),
        scratch_shapes=[pltpu.VMEM((tb, D_QK), F32)],
        compiler_params=_params(),
    )(x, g, w, cos_t, sin_t, xif, xib, zft, zbt)


def _scan_kernel(kvf_ref, kvb_ref, dcf_ref, dcb_ref, sf_ref, sb_ref, stf, stb):
    @pl.when(pl.program_id(1) == 0)
    def _():
        stf[...] = jnp.zeros_like(stf)
        stb[...] = jnp.zeros_like(stb)

    for kv_ref, dc_ref, s_ref, st in ((kvf_ref, dcf_ref, sf_ref, stf), (kvb_ref, dcb_ref, sb_ref, stb)):
        cur = st[...]
        s_ref[:, 0] = cur.astype(BF16)
        st[...] = cur * dc_ref[...][None] + kv_ref[:, 0]


def _scan(kvf, kvb, dcf, dcb, batch):
    nch = kvf.shape[0] // batch
    bb = min(batch, 8)
    shape5 = (batch, nch, PAIRS, 2 * DK, DV)
    blk = (bb, 1, PAIRS, 2 * DK, DV)
    fwd = lambda b, i: (b, i, 0, 0, 0)
    bwd = lambda b, i: (b, nch - 1 - i, 0, 0, 0)
    const3 = lambda b, i: (0, 0, 0)
    sf, sb = pl.pallas_call(
        _scan_kernel,
        grid=(batch // bb, nch),
        in_specs=[
            pl.BlockSpec(blk, fwd),
            pl.BlockSpec(blk, bwd),
            pl.BlockSpec((PAIRS, 2 * DK, DV), const3),
            pl.BlockSpec((PAIRS, 2 * DK, DV), const3),
        ],
        out_specs=(pl.BlockSpec(blk, fwd), pl.BlockSpec(blk, bwd)),
        out_shape=(jax.ShapeDtypeStruct(shape5, BF16), jax.ShapeDtypeStruct(shape5, BF16)),
        scratch_shapes=[pltpu.VMEM((bb, PAIRS, 2 * DK, DV), F32), pltpu.VMEM((bb, PAIRS, 2 * DK, DV), F32)],
        compiler_params=_params(2),
    )(kvf.reshape(shape5), kvb.reshape(shape5), dcf, dcb)
    flat = (batch * nch, PAIRS, 2 * DK, DV)
    return sf.reshape(flat), sb.reshape(flat)


def _mixer_kernel(seq_len, proj_ref, uprev_ref, unext_ref, x_ref, sf_ref, sb_ref, m_ref, gn_ref,
                  pw_ref, ps_ref, wout_ref, gpost_ref, out_ref, yr_scr, u_scr, m_scr):
    tb = x_ref.shape[0]
    c = CHUNK
    lane = lax.broadcasted_iota(jnp.int32, (c, 128), 1)
    lane_even = (((lane >> 5) & 1) == 0)
    q_mask = (lane_even.astype(F32).astype(BF16), (1.0 - lane_even.astype(F32)).astype(BF16))
    row = lax.broadcasted_iota(jnp.int32, (2 * DK, DV), 0)
    row_even = (((row >> 5) & 1) == 0)
    s_mask = (row_even.astype(F32).astype(BF16), (1.0 - row_even.astype(F32)).astype(BF16))

    def chunk_body(ci, carry):
        r0 = pl.multiple_of(ci * c, c)
        rows = pl.ds(r0, c)
        for p in range(PAIRS):
            lo = p * 128
            q = proj_ref[rows, OFF_Q + lo:OFF_Q + lo + 128]
            k = proj_ref[rows, OFF_K + lo:OFF_K + lo + 128]
            qf = proj_ref[rows, OFF_QF + lo:OFF_QF + lo + 128]
            qb = proj_ref[rows, OFF_QB + lo:OFF_QB + lo + 128]
            a = jnp.concatenate([q * q_mask[0], q * q_mask[1]], axis=0)
            s = lax.dot_general(a, k, (((1,), (1,)), ((), ())), preferred_element_type=F32)
            wf = sf_ref[ci, p]
            wb = sb_ref[ci, p]
            w_all = jnp.concatenate(
                [jnp.concatenate([wf * s_mask[0], wf * s_mask[1]], axis=1),
                 jnp.concatenate([wb * s_mask[0], wb * s_mask[1]], axis=1)], axis=0)
            inter = jnp.dot(jnp.concatenate([qf, qb], axis=1), w_all, preferred_element_type=F32)
            for hh in range(2):
                h = 2 * p + hh
                hl = h * DV
                pm = (s[hh * c:(hh + 1) * c] * m_ref[h]).astype(BF16)
                v = proj_ref[rows, OFF_V + hl:OFF_V + hl + DV]
                o = jnp.dot(pm, v, preferred_element_type=F32) + inter[:, hh * DV:(hh + 1) * DV]
                mu = jnp.mean(o, axis=-1, keepdims=True)
                d = o - mu
                var = jnp.mean(d * d, axis=-1, keepdims=True)
                on = d * lax.rsqrt(var + EPS) * gn_ref[:, hl:hl + DV]
                g = proj_ref[rows, OFF_G + hl:OFF_G + hl + DV].astype(F32)
                gr = proj_ref[rows, OFF_GR + hl:OFF_GR + hl + DV].astype(F32)
                yr_scr[rows, hl:hl + DV] = jax.nn.sigmoid(gr) * (g * jax.nn.sigmoid(g) * on)
        return carry

    lax.fori_loop(0, tb // c, chunk_body, 0)

    i = pl.program_id(0)
    s0 = lax.rem(i * tb, seq_len)
    u = proj_ref[:, OFF_U:OFF_U + D_MODEL].astype(F32)
    u_scr[HALO:HALO + tb, :] = u
    keep_prev = (s0 > 0).astype(F32)
    keep_next = (s0 + tb < seq_len).astype(F32)
    u_scr[0:HALO, :] = uprev_ref[...].astype(F32) * keep_prev
    u_scr[HALO + tb:2 * HALO + tb, :] = unext_ref[...].astype(F32) * keep_next
    pos = s0 + lax.broadcasted_iota(jnp.int32, (tb, 128), 0)
    for gi, w in enumerate(POOL_WINDOWS):
        half = w // 2
        lo = gi * POOL_GW
        acc = u_scr[HALO - half:HALO - half + tb, lo:lo + POOL_GW]
        for dlt in range(-half + 1, half):
            acc = acc + u_scr[HALO + dlt:HALO + dlt + tb, lo:lo + POOL_GW]
        cnt = (jnp.minimum(pos + half, seq_len) - jnp.maximum(pos - half, 0)).astype(F32)
        inv = 1.0 / cnt
        pg = acc * jnp.concatenate([inv, inv], axis=1) - u_scr[HALO:HALO + tb, lo:lo + POOL_GW]
        yp = jnp.dot(pg.astype(BF16), pw_ref[gi], preferred_element_type=F32) * ps_ref[:, lo:lo + POOL_GW]
        gp = proj_ref[:, OFF_GP + lo:OFF_GP + lo + POOL_GW].astype(F32)
        m_scr[:, lo:lo + POOL_GW] = (yr_scr[:, lo:lo + POOL_GW] + jax.nn.sigmoid(gp) * yp).astype(BF16)

    z = jnp.dot(m_scr[...], wout_ref[...], preferred_element_type=F32)
    out_ref[...] = x_ref[...] + _rms(z, gpost_ref[...])


def _mixer(proj, x, sf, sb, m_tab, gn, pw, ps, wout, gpost, seq_len):
    t = x.shape[0]
    tb = TB_MIX
    cpb = tb // CHUNK
    hb = tb // HALO
    n_halo = t // HALO
    u_col = OFF_U // D_MODEL
    const2 = lambda i: (0, 0)
    const3 = lambda i: (0, 0, 0)
    return pl.pallas_call(
        functools.partial(_mixer_kernel, seq_len),
        grid=(t // tb,),
        in_specs=[
            pl.BlockSpec((tb, D_PROJ), lambda i: (i, 0)),
            pl.BlockSpec((HALO, D_MODEL), lambda i: (jnp.maximum(i * hb - 1, 0), u_col)),
            pl.BlockSpec((HALO, D_MODEL), lambda i: (jnp.minimum((i + 1) * hb, n_halo - 1), u_col)),
            pl.BlockSpec((tb, D_MODEL), lambda i: (i, 0)),
            pl.BlockSpec((cpb, PAIRS, 2 * DK, DV), lambda i: (i, 0, 0, 0)),
            pl.BlockSpec((cpb, PAIRS, 2 * DK, DV), lambda i: (i, 0, 0, 0)),
            pl.BlockSpec((HEADS, CHUNK, CHUNK), const3),
            pl.BlockSpec((1, D_MODEL), const2),
            pl.BlockSpec((len(POOL_WINDOWS), POOL_GW, POOL_GW), const3),
            pl.BlockSpec((1, D_MODEL), const2),
            pl.BlockSpec((D_MODEL, D_MODEL), const2),
            pl.BlockSpec((1, D_MODEL), const2),
        ],
        out_specs=pl.BlockSpec((tb, D_MODEL), lambda i: (i, 0)),
        out_shape=jax.ShapeDtypeStruct((t, D_MODEL), F32),
        scratch_shapes=[
            pltpu.VMEM((tb, D_MODEL), F32),
            pltpu.VMEM((tb + 2 * HALO, D_MODEL), F32),
            pltpu.VMEM((tb, D_MODEL), BF16),
        ],
        compiler_params=_params(),
    )(proj, proj, proj, x, sf, sb, m_tab, gn, pw, ps, wout, gpost)


def _mlp_kernel(x_ref, gpre_ref, w1_ref, w2_ref, gpost_ref, out_ref):
    x = x_ref[...]
    h = _rms(x, gpre_ref[...]).astype(BF16)
    acc = jnp.zeros(x.shape, F32)
    for j in range(D_FF // D_MODEL):
        lo = j * D_MODEL
        f = jnp.dot(h, w1_ref[:, lo:lo + D_MODEL], preferred_element_type=F32)
        f = jnp.square(jnp.maximum(f, 0.0)).astype(BF16)
        acc = acc + jnp.dot(f, w2_ref[lo:lo + D_MODEL, :], preferred_element_type=F32)
    out_ref[...] = x + _rms(acc, gpost_ref[...])


def _mlp(x, gpre, w1, w2, gpost):
    t = x.shape[0]
    tb = TB_MLP
    const2 = lambda i: (0, 0)
    return pl.pallas_call(
        _mlp_kernel,
        grid=(t // tb,),
        in_specs=[
            pl.BlockSpec((tb, D_MODEL), lambda i: (i, 0)),
            pl.BlockSpec((1, D_MODEL), const2),
            pl.BlockSpec((D_MODEL, D_FF), const2),
            pl.BlockSpec((D_FF, D_MODEL), const2),
            pl.BlockSpec((1, D_MODEL), const2),
        ],
        out_specs=pl.BlockSpec((tb, D_MODEL), lambda i: (i, 0)),
        out_shape=jax.ShapeDtypeStruct((t, D_MODEL), F32),
        compiler_params=_params(),
    )(x, gpre, w1, w2, gpost)


def _qk_column_order():
    n = np.arange(D_QK)
    pair, l = n // 128, n % 128
    head = 2 * pair + (l // 32) % 2
    return head * DK + (l // 64) * (DK // 2) + l % 32


def _rope_tables(seq_len):
    half = DK // 2
    inv = ROPE_BASE ** (-jnp.arange(half, dtype=F32) / half)
    ang = jnp.arange(seq_len, dtype=F32)[:, None] * inv[None, :]
    cos, sin = jnp.cos(ang), jnp.sin(ang)
    return jnp.tile(cos, (1, 4)), jnp.concatenate([-sin, -sin, sin, sin], axis=1)


def _trunk(x, layers, tabs, cos_t, sin_t):
    batch, seq_len, _ = x.shape
    xt = x.reshape(batch * seq_len, D_MODEL)
    m_tab, xif, xib, zft, zbt, dcf, dcb = tabs
    for l, lw in enumerate(layers):
        proj, kvf, kvb = _inproj(xt, lw["g_mix_pre"], lw["w_in"], cos_t, sin_t,
                                 xif[l], xib[l], zft[l], zbt[l], seq_len)
        sf, sb = _scan(kvf, kvb, dcf[l], dcb[l], batch)
        xt = _mixer(proj, xt, sf, sb, m_tab[l], lw["gn"], lw["pool_w"], lw["pool_scale"],
                    lw["w_out"], lw["g_mix_post"], seq_len)
        xt = _mlp(xt, lw["g_mlp_pre"], lw["w_mlp1"], lw["w_mlp2"], lw["g_mlp_post"])
    return xt.reshape(batch, seq_len, D_MODEL)


def kernel(x_prompt, x_sample, norm_mix_pre, norm_mix_post, w_in, ret_decay_fwd, ret_decay_bwd, ret_gn,
           pool_w, pool_scale, w_out, norm_mlp_pre, norm_mlp_post, w_mlp1, w_mlp2):
    order = _qk_column_order()
    cols = np.concatenate([order, D_QK + order, np.arange(2 * D_QK, w_in.shape[-1])])
    w_in_p = w_in[:, :, cols].astype(BF16)
    layers = []
    for l in range(DEPTH):
        layers.append(dict(
            g_mix_pre=norm_mix_pre[l][None], g_mix_post=norm_mix_post[l][None],
            w_in=w_in_p[l], gn=ret_gn[l][None], pool_w=pool_w[l].astype(BF16),
            pool_scale=pool_scale[l][None], w_out=w_out[l].astype(BF16),
            g_mlp_pre=norm_mlp_pre[l][None], g_mlp_post=norm_mlp_post[l][None],
            w_mlp1=w_mlp1[l].astype(BF16), w_mlp2=w_mlp2[l].astype(BF16)))
    dec = jnp.concatenate([ret_decay_fwd, ret_decay_bwd], axis=1).astype(F32)
    tabs = _decay_tables(dec)
    cos_t, sin_t = _rope_tables(max(x_prompt.shape[1], x_sample.shape[1]))
    y_prompt = _trunk(x_prompt, layers, tabs, cos_t, sin_t)
    y_sample = _trunk(x_sample, layers, tabs, cos_t, sin_t)
    return (y_prompt, y_sample)
```

```python
import functools

import numpy as np
import jax
import jax.numpy as jnp
from jax import lax
from jax.experimental import pallas as pl
from jax.experimental.pallas import tpu as pltpu

D_MODEL = 1024
DEPTH = 4
HEADS = 8
DK = 64
DV = 128
PAIRS = HEADS // 2
D_QK = HEADS * DK
CHUNK = 128
ROPE_BASE = 10000.0
POOL_WINDOWS = (2, 4, 8, 16)
POOL_GW = 256
D_FF = 4 * D_MODEL
EPS = 1e-6
QK_SCALE = DK ** -0.5

OFF_U, OFF_V, OFF_A, OFF_SGP, OFF_Q, OFF_QF, OFF_QB = 0, 1024, 2048, 3072, 4096, 4608, 5120
D_PROJ = 5632
W_V, W_G, W_U, W_GR, W_GP = 0, 1024, 2048, 3072, 4096
HALO = 16
POOL_SUB = 128
POOL_WIN = 256

TB_IN = 512
TB_MIX = 512
TB_MLP = 512
VMEM_LIMIT_BYTES = 56 * 1024 * 1024

F32 = jnp.float32
BF16 = jnp.bfloat16


def _rms(x, g):
    return x * lax.rsqrt(jnp.mean(x * x, axis=-1, keepdims=True) + EPS) * g


def _log_sigmoid(d):
    return jnp.minimum(d, 0.0) - jnp.log1p(jnp.exp(-jnp.abs(d)))


def _params():
    return pltpu.CompilerParams(dimension_semantics=("arbitrary",), vmem_limit_bytes=VMEM_LIMIT_BYTES)


def _tables_kernel(dec_ref, m_ref, xif_ref, xib_ref, zft_ref, zbt_ref, dcf_ref, dcb_ref):
    l = pl.program_id(0)
    c = CHUNK
    df = [dec_ref[l, h] for h in range(HEADS)]
    db = [dec_ref[l, HEADS + h] for h in range(HEADS)]

    ii = lax.broadcasted_iota(jnp.int32, (c, c), 0)
    jj = lax.broadcasted_iota(jnp.int32, (c, c), 1)
    dist = jnp.abs(ii - jj).astype(F32)
    for h in range(HEADS):
        dsel = jnp.where(ii >= jj, jnp.full((c, c), df[h], F32), jnp.full((c, c), db[h], F32))
        m_ref[0, h // 2, :, (h % 2) * c:(h % 2 + 1) * c] = jnp.exp(dist * _log_sigmoid(dsel))

    rows = xif_ref.shape[1]
    lane = lax.broadcasted_iota(jnp.int32, (rows, D_QK), 1)
    head = 2 * (lane >> 7) + ((lane >> 5) & 1)
    idx = (lax.broadcasted_iota(jnp.int32, (rows, D_QK), 0) & (c - 1)).astype(F32)
    sel_f = jnp.full((rows, D_QK), df[0], F32)
    sel_b = jnp.full((rows, D_QK), db[0], F32)
    for h in range(1, HEADS):
        sel_f = jnp.where(head == h, df[h], sel_f)
        sel_b = jnp.where(head == h, db[h], sel_b)
    xif_ref[0] = jnp.exp((idx + 1.0) * _log_sigmoid(sel_f))
    xib_ref[0] = jnp.exp((c - idx) * _log_sigmoid(sel_b))

    rl = lax.broadcasted_iota(jnp.int32, (2 * DK, c), 0)
    odd = ((rl >> 5) & 1) == 1
    jc = lax.broadcasted_iota(jnp.int32, (2 * DK, c), 1).astype(F32)
    for p in range(PAIRS):
        lg_f = _log_sigmoid(jnp.where(odd, df[2 * p + 1], df[2 * p]))
        lg_b = _log_sigmoid(jnp.where(odd, db[2 * p + 1], db[2 * p]))
        zft_ref[0, p] = jnp.exp((c - 1.0 - jc) * lg_f)
        zbt_ref[0, p] = jnp.exp(jc * lg_b)
        dcf_ref[0, p] = jnp.exp(float(c) * lg_f)
        dcb_ref[0, p] = jnp.exp(float(c) * lg_b)


def _decay_tables(dec):
    c = CHUNK
    pair_tab = jax.ShapeDtypeStruct((DEPTH, PAIRS, 2 * DK, c), F32)
    pair_spec = pl.BlockSpec((1, PAIRS, 2 * DK, c), lambda l: (l, 0, 0, 0))
    lane_tab = jax.ShapeDtypeStruct((DEPTH, TB_IN, D_QK), F32)
    lane_spec = pl.BlockSpec((1, TB_IN, D_QK), lambda l: (l, 0, 0))
    return pl.pallas_call(
        _tables_kernel,
        grid=(DEPTH,),
        in_specs=[pl.BlockSpec(memory_space=pltpu.SMEM)],
        out_specs=(pl.BlockSpec((1, PAIRS, c, 2 * c), lambda l: (l, 0, 0, 0)),
                   lane_spec, lane_spec, pair_spec, pair_spec, pair_spec, pair_spec),
        out_shape=(jax.ShapeDtypeStruct((DEPTH, PAIRS, c, 2 * c), F32),
                   lane_tab, lane_tab, pair_tab, pair_tab, pair_tab, pair_tab),
        compiler_params=_params(),
    )(dec)


def _inproj_kernel(seq_len, x_ref, g_ref, wqk_ref, w_ref, cos_ref, sin_ref, xif_ref, xib_ref,
                   zft_ref, zbt_ref, dcf_ref, proj_ref, kt_ref, sf_ref, kvb_ref, k_scr, stf):
    tb = x_ref.shape[0]

    @pl.when(lax.rem(pl.program_id(0) * tb, seq_len) == 0)
    def _():
        stf[...] = jnp.zeros_like(stf)

    h = _rms(x_ref[...], g_ref[...]).astype(BF16)
    cos = cos_ref[...]
    sin = sin_ref[...]

    qk = jnp.dot(h, wqk_ref[...], preferred_element_type=F32)
    for p in range(PAIRS):
        lo = p * 128
        qp = qk[:, lo:lo + 128]
        qr = (qp * cos + pltpu.roll(qp, 64, 1) * sin) * QK_SCALE
        proj_ref[:, OFF_Q + lo:OFF_Q + lo + 128] = qr.astype(BF16)
        proj_ref[:, OFF_QF + lo:OFF_QF + lo + 128] = (qr * xif_ref[:, lo:lo + 128]).astype(BF16)
        proj_ref[:, OFF_QB + lo:OFF_QB + lo + 128] = (qr * xib_ref[:, lo:lo + 128]).astype(BF16)
        kp = qk[:, D_QK + lo:D_QK + lo + 128]
        kr = kp * cos + pltpu.roll(kp, 64, 1) * sin
        k_scr[:, lo:lo + 128] = kr

    def section(w_off):
        return jnp.dot(h, w_ref[:, w_off:w_off + D_MODEL], preferred_element_type=F32)

    proj_ref[:, OFF_V:OFF_V + D_MODEL] = section(W_V).astype(BF16)
    proj_ref[:, OFF_U:OFF_U + D_MODEL] = section(W_U).astype(BF16)
    g = section(W_G)
    proj_ref[:, OFF_A:OFF_A + D_MODEL] = (g * jax.nn.sigmoid(g) * jax.nn.sigmoid(section(W_GR))).astype(BF16)
    proj_ref[:, OFF_SGP:OFF_SGP + D_MODEL] = jax.nn.sigmoid(section(W_GP)).astype(BF16)

    rl = lax.broadcasted_iota(jnp.int32, (2 * DK, DV), 0)
    even_row = ((rl >> 5) & 1) == 0
    for c in range(tb // CHUNK):
        r0 = c * CHUNK
        for p in range(PAIRS):
            kt = k_scr[r0:r0 + CHUNK, p * 128:(p + 1) * 128].T
            kt_ref[c, p] = kt.astype(BF16)
            vp = proj_ref[r0:r0 + CHUNK, OFF_V + 2 * p * DV:OFF_V + (2 * p + 2) * DV]
            a = jnp.concatenate([(kt * zft_ref[p]).astype(BF16), (kt * zbt_ref[p]).astype(BF16)], axis=0)
            kv = jnp.dot(a, vp, preferred_element_type=F32)
            kvf = jnp.where(even_row, kv[0:128, 0:DV], kv[0:128, DV:2 * DV])
            kvb_ref[c, p] = jnp.where(even_row, kv[128:256, 0:DV], kv[128:256, DV:2 * DV])
            cur = stf[p]
            sf_ref[c, p] = cur.astype(BF16)
            stf[p] = cur * dcf_ref[p] + kvf


def _inproj(x, g, wqk, w, cos_t, sin_t, xif, xib, zft, zbt, dcf, seq_len):
    t = x.shape[0]
    tb = TB_IN
    nb_seq = seq_len // tb
    cpb = tb // CHUNK
    const2 = lambda i: (0, 0)
    const3 = lambda i: (0, 0, 0)
    pair_spec = pl.BlockSpec((PAIRS, 2 * DK, CHUNK), const3)
    state_blk = pl.BlockSpec((cpb, PAIRS, 2 * DK, DV), lambda i: (i, 0, 0, 0))
    return pl.pallas_call(
        functools.partial(_inproj_kernel, seq_len),
        grid=(t // tb,),
        in_specs=[
            pl.BlockSpec((tb, D_MODEL), lambda i: (i, 0)),
            pl.BlockSpec((1, D_MODEL), const2),
            pl.BlockSpec(wqk.shape, const2),
            pl.BlockSpec(w.shape, const2),
            pl.BlockSpec((tb, 128), lambda i: (i % nb_seq, 0)),
            pl.BlockSpec((tb, 128), lambda i: (i % nb_seq, 0)),
            pl.BlockSpec((tb, D_QK), const2),
            pl.BlockSpec((tb, D_QK), const2),
            pair_spec, pair_spec, pair_spec,
        ],
        out_specs=(pl.BlockSpec((tb, D_PROJ), lambda i: (i, 0)), state_blk, state_blk, state_blk),
        out_shape=(
            jax.ShapeDtypeStruct((t, D_PROJ), BF16),
            jax.ShapeDtypeStruct((t // CHUNK, PAIRS, 2 * DK, CHUNK), BF16),
            jax.ShapeDtypeStruct((t // CHUNK, PAIRS, 2 * DK, DV), BF16),
            jax.ShapeDtypeStruct((t // CHUNK, PAIRS, 2 * DK, DV), F32),
        ),
        scratch_shapes=[pltpu.VMEM((tb, D_QK), F32), pltpu.VMEM((PAIRS, 2 * DK, DV), F32)],
        compiler_params=_params(),
    )(x, g, wqk, w, cos_t, sin_t, xif, xib, zft, zbt, dcf)


def _mixer_kernel(seq_len, proj_ref, uprev_ref, unext_ref, kt_ref, sf_ref, kvb_ref, dcb_ref, m_ref, band_ref,
                  gn_ref, pw_ref, ps_ref, wout_ref, out_ref, yr_scr, u_scr, m_scr, stb):
    tb = proj_ref.shape[0]
    c = CHUNK
    nblk = pl.num_programs(0)
    blk = nblk - 1 - pl.program_id(0)
    s0 = lax.rem(blk * tb, seq_len)

    @pl.when(s0 + tb == seq_len)
    def _():
        stb[...] = jnp.zeros_like(stb)

    row = lax.broadcasted_iota(jnp.int32, (2 * DK, DV), 0)
    row_even = (((row >> 5) & 1) == 0).astype(F32)
    s_mask = (row_even.astype(BF16), (1.0 - row_even).astype(BF16))
    zeros_v = jnp.zeros((c, DV), BF16)

    for ci in reversed(range(tb // c)):
        rows = slice(ci * c, (ci + 1) * c)
        for p in range(PAIRS):
            lo = p * 128
            q = proj_ref[rows, OFF_Q + lo:OFF_Q + lo + 128]
            kt = kt_ref[ci, p]
            qf =proj_ref[rows, OFF_QF + lo:OFF_QF + lo + 128]
            qb = proj_ref[rows, OFF_QB + lo:OFF_QB + lo + 128]
            v0 = proj_ref[rows, OFF_V + 2 * lo:OFF_V + 2 * lo + DV]
            v1 = proj_ref[rows, OFF_V + 2 * lo + DV:OFF_V + 2 * lo + 2 * DV]
            kcat = jnp.concatenate([kt * s_mask[0], kt * s_mask[1]], axis=1)
            s = jnp.dot(q, kcat, preferred_element_type=F32)
            pm =(s * m_ref[p]).astype(BF16)
            wf = sf_ref[ci, p]
            cur_b = stb[p]
            wb = cur_b.astype(BF16)
            stb[p] = cur_b * dcb_ref[p] + kvb_ref[ci, p]
            rhs = jnp.concatenate(
                [jnp.concatenate([v0, zeros_v], axis=1),
                 jnp.concatenate([zeros_v, v1], axis=1),
                 jnp.concatenate([wf * s_mask[0], wf * s_mask[1]], axis=1),
                 jnp.concatenate([wb * s_mask[0], wb * s_mask[1]], axis=1)], axis=0)
            o2 = jnp.dot(jnp.concatenate([pm, qf, qb], axis=1), rhs, preferred_element_type=F32)
            for hh in range(2):
                hl = (2 * p + hh) * DV
                o = o2[:, hh * DV:(hh + 1) * DV]
                mu = jnp.mean(o, axis=-1, keepdims=True)
                d = o - mu
                var = jnp.mean(d * d, axis=-1, keepdims=True)
                on = d * lax.rsqrt(var + EPS) * gn_ref[:, hl:hl + DV]
                yr_scr[rows, hl:hl + DV] = proj_ref[rows, OFF_A + hl:OFF_A + hl + DV].astype(F32) * on

    keep_prev = jnp.where(s0 > 0, 1.0, 0.0)
    keep_next = jnp.where(s0 + tb < seq_len, 1.0, 0.0)
    u_scr[0:HALO, :] = (uprev_ref[...].astype(F32) * keep_prev).astype(BF16)
    u_scr[HALO:HALO + tb, :] = proj_ref[:, OFF_U:OFF_U + D_MODEL]
    u_scr[HALO + tb:2 * HALO + tb, :] = (unext_ref[...].astype(F32) * keep_next).astype(BF16)
    u_scr[2 * HALO + tb:, :] = jnp.zeros((u_scr.shape[0] - 2 * HALO - tb, D_MODEL), BF16)
    nsub = tb // POOL_SUB
    for sb in range(nsub):
        r0 = sb * POOL_SUB
        rows = slice(r0, r0 + POOL_SUB)
        pos = s0 + r0 + lax.broadcasted_iota(jnp.int32, (POOL_SUB, 128), 0)
        for gi, w in enumerate(POOL_WINDOWS):
            half = w // 2
            lo = gi * POOL_GW
            wsum = jnp.dot(band_ref[gi], u_scr[r0:r0 + POOL_WIN, lo:lo + POOL_GW], preferred_element_type=F32)
            if sb in (0, nsub - 1):
                cnt = (jnp.minimum(pos + half, seq_len) - jnp.maximum(pos - half, 0)).astype(F32)
                inv = 1.0 / cnt
                mean = wsum * jnp.concatenate([inv, inv], axis=1)
            else:
                mean = wsum * (1.0 / w)
            pg = mean - proj_ref[rows, OFF_U + lo:OFF_U + lo + POOL_GW].astype(F32)
            yp = jnp.dot(pg.astype(BF16), pw_ref[gi], preferred_element_type=F32) * ps_ref[:, lo:lo + POOL_GW]
            sgp = proj_ref[rows, OFF_SGP + lo:OFF_SGP + lo + POOL_GW].astype(F32)
            m_scr[rows, lo:lo + POOL_GW] = (yr_scr[rows, lo:lo + POOL_GW] + sgp * yp).astype(BF16)

    out_ref[...] = jnp.dot(m_scr[...], wout_ref[...], preferred_element_type=F32)


def _pool_bands():
    i = np.arange(POOL_SUB)[:, None]
    j = np.arange(POOL_WIN)[None, :] - HALO
    return np.stack([((j - i >= -(w // 2)) & (j - i <= w // 2 - 1)) for w in POOL_WINDOWS]).astype(np.float32)


def _mixer(proj, kt, sf, kvb, dcb, m_tab, gn, pw, ps, wout, seq_len):
    t = proj.shape[0]
    tb = TB_MIX
    nb = t // tb
    cpb = tb // CHUNK
    hb = tb // HALO
    n_halo = t // HALO
    u_col = OFF_U // D_MODEL
    band = jnp.asarray(_pool_bands(), BF16)
    rev = lambda i: nb - 1 - i
    const2 = lambda i: (0, 0)
    const3 = lambda i: (0, 0, 0)
    state_blk = pl.BlockSpec((cpb, PAIRS, 2 * DK, DV), lambda i: (rev(i), 0, 0, 0))
    return pl.pallas_call(
        functools.partial(_mixer_kernel, seq_len),
        grid=(nb,),
        in_specs=[
            pl.BlockSpec((tb, D_PROJ), lambda i: (rev(i), 0)),
            pl.BlockSpec((HALO, D_MODEL), lambda i: (jnp.maximum(rev(i) * hb - 1, 0), u_col)),
            pl.BlockSpec((HALO, D_MODEL), lambda i: (jnp.minimum((rev(i) + 1) * hb, n_halo - 1), u_col)),
            state_blk, state_blk, state_blk,
            pl.BlockSpec((PAIRS, 2 * DK, DV), const3),
            pl.BlockSpec((PAIRS, CHUNK, 2 * CHUNK), const3),
            pl.BlockSpec(band.shape, const3),
            pl.BlockSpec((1, D_MODEL), const2),
            pl.BlockSpec((len(POOL_WINDOWS), POOL_GW, POOL_GW), const3),
            pl.BlockSpec((1, D_MODEL), const2),
            pl.BlockSpec((D_MODEL, D_MODEL), const2),
        ],
        out_specs=pl.BlockSpec((tb, D_MODEL), lambda i: (rev(i), 0)),
        out_shape=jax.ShapeDtypeStruct((t, D_MODEL), F32),
        scratch_shapes=[
            pltpu.VMEM((tb, D_MODEL), F32),
            pltpu.VMEM((tb - POOL_SUB + POOL_WIN, D_MODEL), BF16),
            pltpu.VMEM((tb, D_MODEL), BF16),
            pltpu.VMEM((PAIRS, 2 * DK, DV), F32),
        ],
        compiler_params=_params(),
    )(proj, proj, proj, kt, sf, kvb, dcb, m_tab, band, gn, pw, ps, wout)


def _mlp_kernel(x_ref, z_ref, gmix_ref, gpre_ref, w1_ref, w2_ref, gpost_ref, out_ref):
    x = x_ref[...] + _rms(z_ref[...], gmix_ref[...])
    h = _rms(x, gpre_ref[...]).astype(BF16)
    acc = jnp.zeros(x.shape, F32)
    for j in range(D_FF // D_MODEL):
        lo = j * D_MODEL
        f = jnp.dot(h, w1_ref[:, lo:lo + D_MODEL], preferred_element_type=F32)
        f = jnp.square(jnp.maximum(f, 0.0)).astype(BF16)
        acc = acc + jnp.dot(f, w2_ref[lo:lo + D_MODEL, :], preferred_element_type=F32)
    out_ref[...] = x + _rms(acc, gpost_ref[...])


def _mlp(x, z, gmix, gpre, w1, w2, gpost):
    t = x.shape[0]
    tb = TB_MLP
    const2 = lambda i: (0, 0)
    return pl.pallas_call(
        _mlp_kernel,
        grid=(t // tb,),
        in_specs=[
            pl.BlockSpec((tb, D_MODEL), lambda i: (i, 0)),
            pl.BlockSpec((tb, D_MODEL), lambda i: (i, 0)),
            pl.BlockSpec((1, D_MODEL), const2),
            pl.BlockSpec((1, D_MODEL), const2),
            pl.BlockSpec((D_MODEL, D_FF), const2),
            pl.BlockSpec((D_FF, D_MODEL), const2),
            pl.BlockSpec((1, D_MODEL), const2),
        ],
        out_specs=pl.BlockSpec((tb, D_MODEL), lambda i: (i, 0)),
        out_shape=jax.ShapeDtypeStruct((t, D_MODEL), F32),
        compiler_params=_params(),
    )(x, z, gmix, gpre, w1, w2, gpost)


def _qk_column_order():
    n = np.arange(D_QK)
    pair, l = n // 128, n % 128
    head = 2 * pair + (l // 32) % 2
    return head * DK + (l // 64) * (DK // 2) + l % 32


def _rope_tables(seq_len):
    half = DK // 2
    inv = ROPE_BASE ** (-jnp.arange(half, dtype=F32) / half)
    ang = jnp.arange(seq_len, dtype=F32)[:, None] * inv[None, :]
    cos, sin = jnp.cos(ang), jnp.sin(ang)
    return jnp.tile(cos, (1, 4)), jnp.concatenate([-sin, -sin, sin, sin], axis=1)


def _trunk(x, layers, tabs, cos_t, sin_t):
    batch, seq_len, _ = x.shape
    xt = x.reshape(batch * seq_len, D_MODEL)
    m_tab, xif, xib, zft, zbt, dcf, dcb = tabs
    for l, lw in enumerate(layers):
        proj, kt, sf, kvb = _inproj(xt, lw["g_mix_pre"], lw["w_qk"], lw["w_rest"], cos_t, sin_t,
                                    xif[l], xib[l], zft[l], zbt[l], dcf[l], seq_len)
        z = _mixer(proj, kt, sf, kvb, dcb[l], m_tab[l], lw["gn"], lw["pool_w"], lw["pool_scale"],
                   lw["w_out"], seq_len)
        xt = _mlp(xt, z, lw["g_mix_post"], lw["g_mlp_pre"], lw["w_mlp1"], lw["w_mlp2"], lw["g_mlp_post"])
    return xt.reshape(batch, seq_len, D_MODEL)


def kernel(x_prompt, x_sample, norm_mix_pre, norm_mix_post, w_in, ret_decay_fwd, ret_decay_bwd, ret_gn,
           pool_w, pool_scale, w_out, norm_mlp_pre, norm_mlp_post, w_mlp1, w_mlp2):
    order = _qk_column_order()
    qk_cols = np.concatenate([order, D_QK + order])
    w_qk = w_in[:, :, :2 * D_QK][:, :, qk_cols].astype(BF16)
    w_rest = w_in[:, :, 2 * D_QK:].astype(BF16)
    layers = []
    for l in range(DEPTH):
        layers.append(dict(
            g_mix_pre=norm_mix_pre[l][None], g_mix_post=norm_mix_post[l][None],
            w_qk=w_qk[l], w_rest=w_rest[l], gn=ret_gn[l][None], pool_w=pool_w[l].astype(BF16),
            pool_scale=pool_scale[l][None], w_out=w_out[l].astype(BF16),
            g_mlp_pre=norm_mlp_pre[l][None], g_mlp_post=norm_mlp_post[l][None],
            w_mlp1=w_mlp1[l].astype(BF16), w_mlp2=w_mlp2[l].astype(BF16)))
    dec = jnp.concatenate([ret_decay_fwd, ret_decay_bwd], axis=1).astype(F32)
    tabs = _decay_tables(dec)
    cos_t, sin_t = _rope_tables(max(x_prompt.shape[1], x_sample.shape[1]))
    y_prompt = _trunk(x_prompt, layers, tabs, cos_t, sin_t)
    y_sample = _trunk(x_sample, layers, tabs, cos_t, sin_t)
    return (y_prompt, y_sample)
```

```python
import functools

import numpy as np
import jax
import jax.numpy as jnp
from jax import lax
from jax.experimental import pallas as pl
from jax.experimental.pallas import tpu as pltpu

D_MODEL = 1024
DEPTH = 4
HEADS = 8
DK = 64
DV = 128
PAIRS = HEADS // 2
D_QK = HEADS * DK
CHUNK = 128
ROPE_BASE = 10000.0
POOL_WINDOWS = (2, 4, 8, 16)
POOL_GW = 256
D_FF = 4 * D_MODEL
EPS = 1e-6
QK_SCALE = DK ** -0.5

OFF_U, OFF_V, OFF_A, OFF_SGP, OFF_Q, OFF_QF, OFF_QB = 0, 1024, 2048, 3072, 4096, 4608, 5120
D_PROJ = 5632
W_V, W_G, W_U, W_GR, W_GP = 0, 1024, 2048, 3072, 4096
HALO = 16
POOL_SUB = 128
POOL_WIN = 256
BAND_INTERIOR, BAND_FIRST, BAND_LAST = 0, 1, 2

TB_IN = 512
TB_MIX = 512
WOUT_ROWS, WOUT_COLS = 256, 256
TB_MLP = 1024
MLP_SPLIT = 2
VMEM_LIMIT_BYTES = 56 * 1024 * 1024

F32 = jnp.float32
BF16 = jnp.bfloat16


def _rms(x, g):
    return x * lax.rsqrt(jnp.mean(x * x, axis=-1, keepdims=True) + EPS) * g


def _log_sigmoid(d):
    return jnp.minimum(d, 0.0) - jnp.log1p(jnp.exp(-jnp.abs(d)))


def _params():
    return pltpu.CompilerParams(dimension_semantics=("arbitrary",), vmem_limit_bytes=VMEM_LIMIT_BYTES)


def _tables_kernel(dec_ref, m_ref, xif_ref, xib_ref, zft_ref, zbt_ref, dcf_ref, dcb_ref):
    l = pl.program_id(0)
    c = CHUNK
    df = [dec_ref[l, h] for h in range(HEADS)]
    db = [dec_ref[l, HEADS + h] for h in range(HEADS)]

    ii = lax.broadcasted_iota(jnp.int32, (c, c), 0)
    jj = lax.broadcasted_iota(jnp.int32, (c, c), 1)
    dist = jnp.abs(ii - jj).astype(F32)
    for h in range(HEADS):
        dsel = jnp.where(ii >= jj, jnp.full((c, c), df[h], F32), jnp.full((c, c), db[h], F32))
        m_ref[0, h // 2, :, (h % 2) * c:(h % 2 + 1) * c] = jnp.exp(dist * _log_sigmoid(dsel))

    rows = xif_ref.shape[1]
    lane = lax.broadcasted_iota(jnp.int32, (rows, D_QK), 1)
    head = 2 * (lane >> 7) + ((lane >> 5) & 1)
    idx = (lax.broadcasted_iota(jnp.int32, (rows, D_QK), 0) & (c - 1)).astype(F32)
    sel_f = jnp.full((rows, D_QK), df[0], F32)
    sel_b = jnp.full((rows, D_QK), db[0], F32)
    for h in range(1, HEADS):
        sel_f = jnp.where(head == h, df[h], sel_f)
        sel_b = jnp.where(head == h, db[h], sel_b)
    xif_ref[0] = jnp.exp((idx + 1.0) * _log_sigmoid(sel_f))
    xib_ref[0] = jnp.exp((c - idx) * _log_sigmoid(sel_b))

    rl = lax.broadcasted_iota(jnp.int32, (2 * DK, c), 0)
    odd = ((rl >> 5) & 1) == 1
    jc = lax.broadcasted_iota(jnp.int32, (2 * DK, c), 1).astype(F32)
    for p in range(PAIRS):
        lg_f = _log_sigmoid(jnp.where(odd, df[2 * p + 1], df[2 * p]))
        lg_b = _log_sigmoid(jnp.where(odd, db[2 * p + 1], db[2 * p]))
        zft_ref[0, p] = jnp.exp((c - 1.0 - jc) * lg_f)
        zbt_ref[0, p] = jnp.exp(jc * lg_b)
        dcf_ref[0, p] = jnp.exp(float(c) * lg_f)
        dcb_ref[0, p] = jnp.exp(float(c) * lg_b)


def _decay_tables(dec):
    c = CHUNK
    pair_tab = jax.ShapeDtypeStruct((DEPTH, PAIRS, 2 * DK, c), F32)
    pair_spec = pl.BlockSpec((1, PAIRS, 2 * DK, c), lambda l: (l, 0, 0, 0))
    lane_tab = jax.ShapeDtypeStruct((DEPTH, TB_IN, D_QK), F32)
    lane_spec = pl.BlockSpec((1, TB_IN, D_QK), lambda l: (l, 0, 0))
    return pl.pallas_call(
        _tables_kernel,
        grid=(DEPTH,),
        in_specs=[pl.BlockSpec(memory_space=pltpu.SMEM)],
        out_specs=(pl.BlockSpec((1, PAIRS, c, 2 * c), lambda l: (l, 0, 0, 0)),
                   lane_spec, lane_spec, pair_spec, pair_spec, pair_spec, pair_spec),
        out_shape=(jax.ShapeDtypeStruct((DEPTH, PAIRS, c, 2 * c), F32),
                   lane_tab, lane_tab, pair_tab, pair_tab, pair_tab, pair_tab),
        compiler_params=_params(),
    )(dec)


def _inproj_kernel(seq_len, x_ref, g_ref, wqk_ref, w_ref, cos_ref, sin_ref, xif_ref, xib_ref,
                   zft_ref, zbt_ref, dcf_ref, gn_ref, proj_ref, kt_ref, sf_ref, kvb_ref, k_scr, stf):
    tb = x_ref.shape[0]

    @pl.when(lax.rem(pl.program_id(0) * tb, seq_len) == 0)
    def _():
        stf[...] = jnp.zeros_like(stf)

    h = _rms(x_ref[...], g_ref[...]).astype(BF16)
    cos = cos_ref[...]
    sin = sin_ref[...]

    qk = jnp.dot(h, wqk_ref[...], preferred_element_type=F32)
    for p in range(PAIRS):
        lo = p * 128
        qp = qk[:, lo:lo + 128]
        qr = (qp * cos + pltpu.roll(qp, 64, 1) * sin) * QK_SCALE
        proj_ref[:, OFF_Q + lo:OFF_Q + lo + 128] = qr.astype(BF16)
        proj_ref[:, OFF_QF + lo:OFF_QF + lo + 128] = (qr * xif_ref[:, lo:lo + 128]).astype(BF16)
        proj_ref[:, OFF_QB + lo:OFF_QB + lo + 128] = (qr * xib_ref[:, lo:lo + 128]).astype(BF16)
        kp = qk[:, D_QK + lo:D_QK + lo + 128]
        kr = kp * cos + pltpu.roll(kp, 64, 1) * sin
        k_scr[:, lo:lo + 128] = kr

    def section(w_off):
        return jnp.dot(h, w_ref[:, w_off:w_off + D_MODEL], preferred_element_type=F32)

    v = section(W_V).astype(BF16)
    proj_ref[:, OFF_V:OFF_V + D_MODEL] = v

    rl = lax.broadcasted_iota(jnp.int32, (2 * DK, DV), 0)
    even_row = ((rl >> 5) & 1) == 0

    def expand(w):
        return jnp.concatenate([jnp.where(even_row, w, 0.0), jnp.where(even_row, 0.0, w)], axis=1).astype(BF16)

    def summaries(c):
        r0 = c * CHUNK
        for p in range(PAIRS):
            kt = k_scr[r0:r0 + CHUNK, p * 128:(p + 1) * 128].T
            kt_ref[c, p] = expand(kt)
            vp = v[r0:r0 + CHUNK, 2 * p * DV:(2 * p + 2) * DV]
            a = jnp.concatenate([(kt * zft_ref[p]).astype(BF16), (kt * zbt_ref[p]).astype(BF16)], axis=0)
            kv = jnp.dot(a, vp, preferred_element_type=F32)
            kvf = jnp.where(even_row, kv[0:128, 0:DV], kv[0:128, DV:2 * DV])
            kvb_ref[c, p] = jnp.where(even_row, kv[128:256, 0:DV], kv[128:256, DV:2 * DV])
            cur = stf[p]
            sf_ref[c, p] = expand(cur)
            stf[p] = cur * dcf_ref[p] + kvf

    proj_ref[:, OFF_U:OFF_U + D_MODEL] = section(W_U).astype(BF16)
    g = section(W_G)
    a_gate = g * jax.nn.sigmoid(g) * jax.nn.sigmoid(section(W_GR)) * gn_ref[...]
    proj_ref[:, OFF_A:OFF_A + D_MODEL] = a_gate.astype(BF16)
    proj_ref[:, OFF_SGP:OFF_SGP + D_MODEL] = jax.nn.sigmoid(section(W_GP)).astype(BF16)
    for c in range(tb // CHUNK):
        summaries(c)


def _inproj(x, g, wqk, w, cos_t, sin_t, xif, xib, zft, zbt, dcf, gn, seq_len):
    t = x.shape[0]
    tb = TB_IN
    nb_seq = seq_len // tb
    cpb = tb // CHUNK
    const2 = lambda i: (0, 0)
    const3 = lambda i: (0, 0, 0)
    pair_spec = pl.BlockSpec((PAIRS, 2 * DK, CHUNK), const3)
    state_blk = pl.BlockSpec((cpb, PAIRS, 2 * DK, DV), lambda i: (i, 0, 0, 0))
    wide_blk = pl.BlockSpec((cpb, PAIRS, 2 * DK, 2 * DV), lambda i: (i, 0, 0, 0))
    return pl.pallas_call(
        functools.partial(_inproj_kernel, seq_len),
        grid=(t // tb,),
        in_specs=[
            pl.BlockSpec((tb, D_MODEL), lambda i: (i, 0)),
            pl.BlockSpec((1, D_MODEL), const2),
            pl.BlockSpec(wqk.shape, const2),
            pl.BlockSpec(w.shape, const2),
            pl.BlockSpec((tb, 128), lambda i: (i % nb_seq, 0)),
            pl.BlockSpec((tb, 128), lambda i: (i % nb_seq, 0)),
            pl.BlockSpec((tb, D_QK), const2),
            pl.BlockSpec((tb, D_QK), const2),
            pair_spec, pair_spec, pair_spec,
            pl.BlockSpec((1, D_MODEL), const2),
        ],
        out_specs=(pl.BlockSpec((tb, D_PROJ), lambda i: (i, 0)), wide_blk, wide_blk, state_blk),
        out_shape=(
            jax.ShapeDtypeStruct((t, D_PROJ), BF16),
            jax.ShapeDtypeStruct((t // CHUNK, PAIRS, 2 * DK, 2 * CHUNK), BF16),
            jax.ShapeDtypeStruct((t // CHUNK, PAIRS, 2 * DK, 2 * DV), BF16),
            jax.ShapeDtypeStruct((t // CHUNK, PAIRS, 2 * DK, DV), F32),
        ),
        scratch_shapes=[pltpu.VMEM((tb, D_QK), F32), pltpu.VMEM((PAIRS, 2 * DK, DV), F32)],
        compiler_params=_params(),
    )(x, g, wqk, w, cos_t, sin_t, xif, xib, zft, zbt, dcf, gn)


def _mixer_kernel(seq_len, proj_ref, uprev_ref, unext_ref, kt_ref, sf_ref, kvb_ref, dcb_ref, m_ref, band_ref,
                  pw_ref, ps_ref, wout_ref, out_ref, yr_scr, yp_scr, u_scr, m_scr, stb):
    tb = proj_ref.shape[0]
    c = CHUNK
    step = pl.program_id(0)
    nblk = pl.num_programs(0) - 1
    blk = nblk - 1 - jnp.minimum(step, nblk - 1)
    s0 = lax.rem(blk * tb, seq_len)
    slot = lax.rem(step, 2)

    @pl.when(step == 0)
    def _():
        m_scr[1] = jnp.zeros(m_scr.shape[1:], BF16)

    @pl.when(s0 + tb == seq_len)
    def _():
        stb[...] = jnp.zeros_like(stb)

    def wout_piece(idx):
        rh, nt = divmod(idx, D_MODEL // WOUT_COLS)
        rows = slice(rh * WOUT_ROWS, (rh + 1) * WOUT_ROWS)
        cols = slice(nt * WOUT_COLS, (nt + 1) * WOUT_COLS)
        out_ref[rows, cols] = jnp.dot(m_scr[1 - slot, rows, :], wout_ref[:, cols], preferred_element_type=F32)

    keep_prev = jnp.where(s0 > 0, 1.0, 0.0)
    keep_next = jnp.where(s0 + tb < seq_len, 1.0, 0.0)
    u_scr[0:HALO, :] = (uprev_ref[...].astype(F32) * keep_prev).astype(BF16)
    u_scr[HALO:HALO + tb, :] = proj_ref[:, OFF_U:OFF_U + D_MODEL]
    u_scr[HALO + tb:2 * HALO + tb, :] = (unext_ref[...].astype(F32) * keep_next).astype(BF16)
    u_scr[2 * HALO + tb:, :] = jnp.zeros((u_scr.shape[0] - 2 * HALO - tb, D_MODEL), BF16)
    nsub = tb // POOL_SUB
    first_var = jnp.where(s0 == 0, BAND_FIRST, BAND_INTERIOR)
    last_var = jnp.where(s0 + tb == seq_len, BAND_LAST, BAND_INTERIOR)

    def pool_group(gi):
        lo = gi * POOL_GW
        parts = []
        for sb in range(nsub):
            r0 = sb * POOL_SUB
            var_idx = first_var if sb == 0 else (last_var if sb == nsub - 1 else BAND_INTERIOR)
            pg = jnp.dot(band_ref[var_idx, gi], u_scr[r0:r0 + POOL_WIN, lo:lo + POOL_GW],
                         preferred_element_type=F32)
            parts.append(pg.astype(BF16))
        yp = jnp.dot(jnp.concatenate(parts, axis=0), pw_ref[gi], preferred_element_type=F32)
        yp_scr[:, lo:lo + POOL_GW] = yp * ps_ref[:, lo:lo + POOL_GW]

    row = lax.broadcasted_iota(jnp.int32, (2 * DK, DV), 0)
    row_even = (((row >> 5) & 1) == 0).astype(F32)
    s_mask = (row_even.astype(BF16), (1.0 - row_even).astype(BF16))
    zeros_v = jnp.zeros((c, DV), BF16)
    order = [(ci, p) for ci in reversed(range(tb // c)) for p in range(PAIRS)]

    def scores(j):
        ci, p = order[j]
        q = proj_ref[ci * c:(ci + 1) * c, OFF_Q + p * 128:OFF_Q + (p + 1) * 128]
        s = jnp.dot(q, kt_ref[ci, p], preferred_element_type=F32)
        return (s * m_ref[p]).astype(BF16)

    def values(j, pm):
        ci, p = order[j]
        rows = slice(ci * c, (ci + 1) * c)
        lo = p * 128
        qf = proj_ref[rows, OFF_QF + lo:OFF_QF + lo + 128]
        qb = proj_ref[rows, OFF_QB + lo:OFF_QB + lo + 128]
        v0 = proj_ref[rows, OFF_V + 2 * lo:OFF_V + 2 * lo + DV]
        v1 = proj_ref[rows, OFF_V + 2 * lo + DV:OFF_V + 2 * lo + 2 * DV]
        cur_b = stb[p]
        wb = cur_b.astype(BF16)
        stb[p] = cur_b * dcb_ref[p] + kvb_ref[ci, p]
        rhs = jnp.concatenate(
            [jnp.concatenate([v0, zeros_v], axis=1),
             jnp.concatenate([zeros_v, v1], axis=1),
             sf_ref[ci, p],
             jnp.concatenate([wb * s_mask[0], wb * s_mask[1]], axis=1)], axis=0)
        o2 = jnp.dot(jnp.concatenate([pm, qf, qb], axis=1), rhs, preferred_element_type=F32)
        for hh in range(2):
            hl = (2 * p + hh) * DV
            o = o2[:, hh * DV:(hh + 1) * DV]
            mu = jnp.mean(o, axis=-1, keepdims=True)
            d = o - mu
            var = jnp.mean(d * d, axis=-1, keepdims=True)
            a = proj_ref[rows, OFF_A + hl:OFF_A + hl + DV].astype(F32)
            yr_scr[rows, hl:hl + DV] = a * (d * lax.rsqrt(var + EPS))
        if p == PAIRS - 1:
            sgp = proj_ref[rows, OFF_SGP:OFF_SGP + D_MODEL].astype(F32)
            m_scr[slot, rows, :] = (yr_scr[rows, :] + sgp * yp_scr[rows, :]).astype(BF16)

    n_wout = (tb // WOUT_ROWS) * (D_MODEL // WOUT_COLS)
    fillers = {0: lambda: pool_group(0), 1: lambda: pool_group(1), 2: lambda: pool_group(2),
               3: lambda: pool_group(3)}
    for k in range(1, n_wout - 1):
        fillers[2 + 2 * k] = functools.partial(wout_piece, k)
    assert max(fillers) < len(order)

    wout_piece(0)
    pending = {0: scores(0), 1: scores(1)}
    for j in range(len(order)):
        if j in fillers:
            fillers[j]()
        values(j, pending.pop(j))
        if j + 2 < len(order):
            pending[j + 2] = scores(j + 2)
    wout_piece(n_wout - 1)


def _pool_bands():
    i = np.arange(POOL_SUB)[:, None]
    j = np.arange(POOL_WIN)[None, :] - HALO
    out = np.zeros((3, len(POOL_WINDOWS), POOL_SUB, POOL_WIN), np.float64)
    for gi, w in enumerate(POOL_WINDOWS):
        half = w // 2
        inside = ((j - i >= -half) & (j - i <= half - 1)).astype(np.float64)
        token = (j == i).astype(np.float64)
        counts = {
            BAND_INTERIOR: np.full((POOL_SUB, 1), float(w)),
            BAND_FIRST: (i + half) - np.maximum(i - half, 0),
            BAND_LAST: np.minimum(i + half, POOL_SUB) - (i - half),
        }
        for var, cnt in counts.items():
            out[var, gi] = inside / cnt - token
    return out.astype(np.float32)


def _mixer(proj, kt, sf, kvb, dcb, m_tab, pw, ps, wout, seq_len):
    t = proj.shape[0]
    tb = TB_MIX
    nb = t // tb
    cpb = tb // CHUNK
    hb = tb // HALO
    n_halo = t // HALO
    u_col = OFF_U // D_MODEL
    band = jnp.asarray(_pool_bands(), BF16)
    rev = lambda i: nb - 1 - jnp.minimum(i, nb - 1)
    rev_out = lambda i: nb - 1 - jnp.maximum(i - 1, 0)
    const2 = lambda i: (0, 0)
    const3 = lambda i: (0, 0, 0)
    state_blk = pl.BlockSpec((cpb, PAIRS, 2 * DK, DV), lambda i: (rev(i), 0, 0, 0))
    wide_blk = pl.BlockSpec((cpb, PAIRS, 2 * DK, 2 * DV), lambda i: (rev(i), 0, 0, 0))
    return pl.pallas_call(
        functools.partial(_mixer_kernel, seq_len),
        grid=(nb + 1,),
        in_specs=[
            pl.BlockSpec((tb, D_PROJ), lambda i: (rev(i), 0)),
            pl.BlockSpec((HALO, D_MODEL), lambda i: (jnp.maximum(rev(i) * hb - 1, 0), u_col)),
            pl.BlockSpec((HALO, D_MODEL), lambda i: (jnp.minimum((rev(i) + 1) * hb, n_halo - 1), u_col)),
            wide_blk, wide_blk, state_blk,
            pl.BlockSpec((PAIRS, 2 * DK, DV), const3),
            pl.BlockSpec((PAIRS, CHUNK, 2 * CHUNK), const3),
            pl.BlockSpec(band.shape, lambda i: (0, 0, 0, 0)),
            pl.BlockSpec((len(POOL_WINDOWS), POOL_GW, POOL_GW), const3),
            pl.BlockSpec((1, D_MODEL), const2),
            pl.BlockSpec((D_MODEL, D_MODEL), const2),
        ],
        out_specs=pl.BlockSpec((tb, D_MODEL), lambda i: (rev_out(i), 0)),
        out_shape=jax.ShapeDtypeStruct((t, D_MODEL), F32),
        scratch_shapes=[
            pltpu.VMEM((tb, D_MODEL), F32),
            pltpu.VMEM((tb, D_MODEL), F32),
            pltpu.VMEM((tb - POOL_SUB + POOL_WIN, D_MODEL), BF16),
            pltpu.VMEM((2, tb, D_MODEL), BF16),
            pltpu.VMEM((PAIRS, 2 * DK, DV), F32),
        ],
        compiler_params=_params(),
    )(proj, proj, proj, kt, sf, kvb, dcb, m_tab, band, pw, ps, wout)


def _mlp_kernel(x_ref, z_ref, gmix_ref, gpre_ref, w1_ref, w2_ref, gpost_ref, out_ref):
    half = x_ref.shape[0] // MLP_SPLIT
    for r in range(MLP_SPLIT):
        rows = slice(r * half, (r + 1) * half)
        x = x_ref[rows, :] + _rms(z_ref[rows, :], gmix_ref[...])
        h = _rms(x, gpre_ref[...]).astype(BF16)
        acc = jnp.zeros(x.shape, F32)
        for j in range(D_FF // D_MODEL):
            lo = j * D_MODEL
            f = jnp.dot(h, w1_ref[:, lo:lo + D_MODEL], preferred_element_type=F32)
            f = jnp.square(jnp.maximum(f, 0.0)).astype(BF16)
            acc = acc + jnp.dot(f, w2_ref[lo:lo + D_MODEL, :], preferred_element_type=F32)
        out_ref[rows, :] = x + _rms(acc, gpost_ref[...])


def _mlp(x, z, gmix, gpre, w1, w2, gpost):
    t = x.shape[0]
    tb = TB_MLP
    const2 = lambda i: (0, 0)
    return pl.pallas_call(
        _mlp_kernel,
        grid=(t // tb,),
        in_specs=[
            pl.BlockSpec((tb, D_MODEL), lambda i: (i, 0)),
            pl.BlockSpec((tb, D_MODEL), lambda i: (i, 0)),
            pl.BlockSpec((1, D_MODEL), const2),
            pl.BlockSpec((1, D_MODEL), const2),
            pl.BlockSpec((D_MODEL, D_FF), const2, pipeline_mode=pl.Buffered(1)),
            pl.BlockSpec((D_FF, D_MODEL), const2, pipeline_mode=pl.Buffered(1)),
            pl.BlockSpec((1, D_MODEL), const2),
        ],
        out_specs=pl.BlockSpec((tb, D_MODEL), lambda i: (i, 0)),
        out_shape=jax.ShapeDtypeStruct((t, D_MODEL), F32),
        compiler_params=_params(),
    )(x, z, gmix, gpre, w1, w2, gpost)


def _qk_column_order():
    n = np.arange(D_QK)
    pair, l = n // 128, n % 128
    head = 2 * pair + (l // 32) % 2
    return head * DK + (l // 64) * (DK // 2) + l % 32


def _rope_tables(seq_len):
    half = DK // 2
    inv = ROPE_BASE ** (-jnp.arange(half, dtype=F32) / half)
    ang = jnp.arange(seq_len, dtype=F32)[:, None] * inv[None, :]
    cos, sin = jnp.cos(ang), jnp.sin(ang)
    return jnp.tile(cos, (1, 4)), jnp.concatenate([-sin, -sin, sin, sin], axis=1)


def _trunk(x, layers, tabs, cos_t, sin_t):
    batch, seq_len, _ = x.shape
    xt = x.reshape(batch * seq_len, D_MODEL)
    m_tab, xif, xib, zft, zbt, dcf, dcb = tabs
    for l, lw in enumerate(layers):
        proj, kt, sf, kvb = _inproj(xt, lw["g_mix_pre"], lw["w_qk"], lw["w_rest"], cos_t, sin_t,
                                    xif[l], xib[l], zft[l], zbt[l], dcf[l], lw["gn"], seq_len)
        z = _mixer(proj, kt, sf, kvb, dcb[l], m_tab[l], lw["pool_w"], lw["pool_scale"], lw["w_out"], seq_len)
        xt = _mlp(xt, z, lw["g_mix_post"], lw["g_mlp_pre"], lw["w_mlp1"], lw["w_mlp2"], lw["g_mlp_post"])
    return xt.reshape(batch, seq_len, D_MODEL)


def kernel(x_prompt, x_sample, norm_mix_pre, norm_mix_post, w_in, ret_decay_fwd, ret_decay_bwd, ret_gn,
           pool_w, pool_scale, w_out, norm_mlp_pre, norm_mlp_post, w_mlp1, w_mlp2):
    order = _qk_column_order()
    qk_cols = np.concatenate([order, D_QK + order])
    w_qk = w_in[:, :, :2 * D_QK][:, :, qk_cols].astype(BF16)
    w_rest = w_in[:, :, 2 * D_QK:].astype(BF16)
    layers = []
    for l in range(DEPTH):
        layers.append(dict(
            g_mix_pre=norm_mix_pre[l][None], g_mix_post=norm_mix_post[l][None],
            w_qk=w_qk[l], w_rest=w_rest[l], gn=ret_gn[l][None], pool_w=pool_w[l].astype(BF16),
            pool_scale=pool_scale[l][None], w_out=w_out[l].astype(BF16),
            g_mlp_pre=norm_mlp_pre[l][None], g_mlp_post=norm_mlp_post[l][None],
            w_mlp1=w_mlp1[l].astype(BF16), w_mlp2=w_mlp2[l].astype(BF16)))
    dec = jnp.concatenate([ret_decay_fwd, ret_decay_bwd], axis=1).astype(F32)
    tabs = _decay_tables(dec)
    cos_t, sin_t = _rope_tables(max(x_prompt.shape[1], x_sample.shape[1]))
    y_prompt = _trunk(x_prompt, layers, tabs, cos_t, sin_t)
    y_sample = _trunk(x_sample, layers, tabs, cos_t, sin_t)
    return (y_prompt, y_sample)
```

```python
import functools

import numpy as np
import jax
import jax.numpy as jnp
from jax import lax
from jax.experimental import pallas as pl
from jax.experimental.pallas import tpu as pltpu

D_MODEL = 1024
DEPTH = 4
HEADS = 8
DK = 64
DV = 128
PAIRS = HEADS // 2
D_QK = HEADS * DK
CHUNK = 128
ROPE_BASE = 10000.0
POOL_WINDOWS = (2, 4, 8, 16)
POOL_GW = 256
D_FF = 4 * D_MODEL
EPS = 1e-6
QK_SCALE = DK ** -0.5

OFF_U, OFF_V, OFF_A, OFF_SGP, OFF_Q, OFF_QF, OFF_QB = 0, 1024, 2048, 3072, 4096, 4608, 5120
D_PROJ = 5632
W_V, W_G, W_U, W_GR, W_GP = 0, 1024, 2048, 3072, 4096
HALO = 16
POOL_SUB = 128
POOL_WIN = 256
BAND_INTERIOR, BAND_FIRST, BAND_LAST = 0, 1, 2

TB_IN = 512
TB_MIX = 512
WOUT_ROWS, WOUT_COLS = 256, 256
TB_MLP = 512
MLP_SPLIT = 1
VMEM_LIMIT_BYTES = 56 * 1024 * 1024

F32 = jnp.float32
BF16 = jnp.bfloat16


def _rms(x, g):
    return x * lax.rsqrt(jnp.mean(x * x, axis=-1, keepdims=True) + EPS) * g


def _zero_like_row(row):
    bits = lax.bitcast_convert_type(row, jnp.uint32)
    return lax.bitcast_convert_type((bits >> 16) >> 16, F32)


def _log_sigmoid(d):
    return jnp.minimum(d, 0.0) - jnp.log1p(jnp.exp(-jnp.abs(d)))


def _params():
    return pltpu.CompilerParams(dimension_semantics=("arbitrary",), vmem_limit_bytes=VMEM_LIMIT_BYTES)


def _tables_kernel(dec_ref, m_ref, xif_ref, xib_ref, zft_ref, zbt_ref, dcf_ref, dcb_ref):
    l = pl.program_id(0)
    c = CHUNK
    df = [dec_ref[l, h] for h in range(HEADS)]
    db = [dec_ref[l, HEADS + h] for h in range(HEADS)]

    ii = lax.broadcasted_iota(jnp.int32, (c, c), 0)
    jj = lax.broadcasted_iota(jnp.int32, (c, c), 1)
    dist = jnp.abs(ii - jj).astype(F32)
    for h in range(HEADS):
        dsel = jnp.where(ii >= jj, jnp.full((c, c), df[h], F32), jnp.full((c, c), db[h], F32))
        m_ref[0, h // 2, :, (h % 2) * c:(h % 2 + 1) * c] = jnp.exp(dist * _log_sigmoid(dsel))

    rows = xif_ref.shape[1]
    lane = lax.broadcasted_iota(jnp.int32, (rows, D_QK), 1)
    head = 2 * (lane >> 7) + ((lane >> 5) & 1)
    idx = (lax.broadcasted_iota(jnp.int32, (rows, D_QK), 0) & (c - 1)).astype(F32)
    sel_f = jnp.full((rows, D_QK), df[0], F32)
    sel_b = jnp.full((rows, D_QK), db[0], F32)
    for h in range(1, HEADS):
        sel_f = jnp.where(head == h, df[h], sel_f)
        sel_b = jnp.where(head == h, db[h], sel_b)
    xif_ref[0] = jnp.exp((idx + 1.0) * _log_sigmoid(sel_f))
    xib_ref[0] = jnp.exp((c - idx) * _log_sigmoid(sel_b))

    rl = lax.broadcasted_iota(jnp.int32, (2 * DK, c), 0)
    odd = ((rl >> 5) & 1) == 1
    jc = lax.broadcasted_iota(jnp.int32, (2 * DK, c), 1).astype(F32)
    for p in range(PAIRS):
        lg_f = _log_sigmoid(jnp.where(odd, df[2 * p + 1], df[2 * p]))
        lg_b = _log_sigmoid(jnp.where(odd, db[2 * p + 1], db[2 * p]))
        zft_ref[0, p] = jnp.exp((c - 1.0 - jc) * lg_f)
        zbt_ref[0, p] = jnp.exp(jc * lg_b)
        dcf_ref[0, p] = jnp.exp(float(c) * lg_f)
        dcb_ref[0, p] = jnp.exp(float(c) * lg_b)


def _decay_tables(dec):
    c = CHUNK
    pair_tab = jax.ShapeDtypeStruct((DEPTH, PAIRS, 2 * DK, c), F32)
    pair_spec = pl.BlockSpec((1, PAIRS, 2 * DK, c), lambda l: (l, 0, 0, 0))
    lane_tab = jax.ShapeDtypeStruct((DEPTH, TB_IN, D_QK), F32)
    lane_spec = pl.BlockSpec((1, TB_IN, D_QK), lambda l: (l, 0, 0))
    return pl.pallas_call(
        _tables_kernel,
        grid=(DEPTH,),
        in_specs=[pl.BlockSpec(memory_space=pltpu.SMEM)],
        out_specs=(pl.BlockSpec((1, PAIRS, c, 2 * c), lambda l: (l, 0, 0, 0)),
                   lane_spec, lane_spec, pair_spec, pair_spec, pair_spec, pair_spec),
        out_shape=(jax.ShapeDtypeStruct((DEPTH, PAIRS, c, 2 * c), F32),
                   lane_tab, lane_tab, pair_tab, pair_tab, pair_tab, pair_tab),
        compiler_params=_params(),
    )(dec)


def _inproj_kernel(seq_len, x_ref, g_ref, wqk_ref, w_ref, cos_ref, sin_ref, xif_ref, xib_ref,
                   zft_ref, dcf_ref, gn_ref, proj_ref, kt_ref, sf_ref, k_scr, stf):
    tb = x_ref.shape[0]

    @pl.when(lax.rem(pl.program_id(0) * tb, seq_len) == 0)
    def _():
        stf[...] = jnp.zeros_like(stf)

    h = _rms(x_ref[...], g_ref[...]).astype(BF16)
    cos = cos_ref[...]
    sin = sin_ref[...]

    qk = jnp.dot(h, wqk_ref[...], preferred_element_type=F32)
    for p in range(PAIRS):
        lo = p * 128
        qp = qk[:, lo:lo + 128]
        qr = (qp * cos + pltpu.roll(qp, 64, 1) * sin) * QK_SCALE
        proj_ref[:, OFF_Q + lo:OFF_Q + lo + 128] = qr.astype(BF16)
        proj_ref[:, OFF_QF + lo:OFF_QF + lo + 128] = (qr * xif_ref[:, lo:lo + 128]).astype(BF16)
        proj_ref[:, OFF_QB + lo:OFF_QB + lo + 128] = (qr * xib_ref[:, lo:lo + 128]).astype(BF16)
        kp = qk[:, D_QK + lo:D_QK + lo + 128]
        kr = kp * cos + pltpu.roll(kp, 64, 1) * sin
        k_scr[:, lo:lo + 128] = kr

    def section(w_off):
        return jnp.dot(h, w_ref[:, w_off:w_off + D_MODEL], preferred_element_type=F32)

    v = section(W_V).astype(BF16)
    proj_ref[:, OFF_V:OFF_V + D_MODEL] = v

    rl = lax.broadcasted_iota(jnp.int32, (2 * DK, DV), 0)
    even_row = ((rl >> 5) & 1) == 0

    def summaries(c):
        r0 = c * CHUNK
        for p in range(PAIRS):
            kt = k_scr[r0:r0 + CHUNK, p * 128:(p + 1) * 128].T
            kt_ref[c, p] = kt.astype(BF16)
            vp = v[r0:r0 + CHUNK, 2 * p * DV:(2 * p + 2) * DV]
            kv = jnp.dot((kt * zft_ref[p]).astype(BF16), vp, preferred_element_type=F32)
            cur = stf[p]
            sf_ref[c, p] = cur.astype(BF16)
            stf[p] = cur * dcf_ref[p] + jnp.where(even_row, kv[:, 0:DV], kv[:, DV:2 * DV])

    proj_ref[:, OFF_U:OFF_U + D_MODEL] = section(W_U).astype(BF16)
    g = section(W_G)
    a_gate = g * jax.nn.sigmoid(g) * jax.nn.sigmoid(section(W_GR)) * gn_ref[...]
    proj_ref[:, OFF_A:OFF_A + D_MODEL] = a_gate.astype(BF16)
    proj_ref[:, OFF_SGP:OFF_SGP + D_MODEL] = jax.nn.sigmoid(section(W_GP)).astype(BF16)
    for c in range(tb // CHUNK):
        summaries(c)


def _inproj(x, g, wqk, w, cos_t, sin_t, xif, xib, zft, dcf, gn, seq_len):
    t = x.shape[0]
    tb = TB_IN
    nb_seq = seq_len // tb
    cpb = tb // CHUNK
    const2 = lambda i: (0, 0)
    const3 = lambda i: (0, 0, 0)
    pair_spec = pl.BlockSpec((PAIRS, 2 * DK, CHUNK), const3)
    state_blk = pl.BlockSpec((cpb, PAIRS, 2 * DK, DV), lambda i: (i, 0, 0, 0))
    return pl.pallas_call(
        functools.partial(_inproj_kernel, seq_len),
        grid=(t // tb,),
        in_specs=[
            pl.BlockSpec((tb, D_MODEL), lambda i: (i, 0)),
            pl.BlockSpec((1, D_MODEL), const2),
            pl.BlockSpec(wqk.shape, const2),
            pl.BlockSpec(w.shape, const2),
            pl.BlockSpec((tb, 128), lambda i: (i % nb_seq, 0)),
            pl.BlockSpec((tb, 128), lambda i: (i % nb_seq, 0)),
            pl.BlockSpec((tb, D_QK), const2),
            pl.BlockSpec((tb, D_QK), const2),
            pair_spec, pair_spec,
            pl.BlockSpec((1, D_MODEL), const2),
        ],
        out_specs=(pl.BlockSpec((tb, D_PROJ), lambda i: (i, 0)), state_blk, state_blk),
        out_shape=(
            jax.ShapeDtypeStruct((t, D_PROJ), BF16),
            jax.ShapeDtypeStruct((t // CHUNK, PAIRS, 2 * DK, CHUNK), BF16),
            jax.ShapeDtypeStruct((t // CHUNK, PAIRS, 2 * DK, DV), BF16),
        ),
        scratch_shapes=[pltpu.VMEM((tb, D_QK), F32), pltpu.VMEM((PAIRS, 2 * DK, DV), F32)],
        compiler_params=_params(),
    )(x, g, wqk, w, cos_t, sin_t, xif, xib, zft, dcf, gn)


def _mixer_kernel(seq_len, proj_ref, uprev_ref, unext_ref, kt_ref, sf_ref, zbt_ref, dcb_ref, m_ref, band_ref,
                  pw_ref, ps_ref, wout_ref, out_ref, yr_scr, yp_scr, u_scr, m_scr, stb):
    tb = proj_ref.shape[0]
    c = CHUNK
    step = pl.program_id(0)
    nblk = pl.num_programs(0) - 1
    blk = nblk - 1 - jnp.minimum(step, nblk - 1)
    s0 = lax.rem(blk * tb, seq_len)
    slot = lax.rem(step, 2)

    @pl.when(step == 0)
    def _():
        m_scr[1] = jnp.zeros(m_scr.shape[1:], BF16)

    @pl.when(s0 + tb == seq_len)
    def _():
        stb[...] = jnp.zeros_like(stb)

    def wout_piece(idx):
        rh, nt = divmod(idx, D_MODEL // WOUT_COLS)
        rows = slice(rh * WOUT_ROWS, (rh + 1) * WOUT_ROWS)
        cols = slice(nt * WOUT_COLS, (nt + 1) * WOUT_COLS)
        out_ref[rows, cols] = jnp.dot(m_scr[1 - slot, rows, :], wout_ref[:, cols], preferred_element_type=F32)

    keep_prev = jnp.where(s0 > 0, 1.0, 0.0)
    keep_next = jnp.where(s0 + tb < seq_len, 1.0, 0.0)
    u_scr[0:HALO, :] = (uprev_ref[...].astype(F32) * keep_prev).astype(BF16)
    u_scr[HALO:HALO + tb, :] = proj_ref[:, OFF_U:OFF_U + D_MODEL]
    u_scr[HALO + tb:2 * HALO + tb, :] = (unext_ref[...].astype(F32) * keep_next).astype(BF16)
    u_scr[2 * HALO + tb:, :] = jnp.zeros((u_scr.shape[0] - 2 * HALO - tb, D_MODEL), BF16)
    nsub = tb // POOL_SUB
    first_var = jnp.where(s0 == 0, BAND_FIRST, BAND_INTERIOR)
    last_var = jnp.where(s0 + tb == seq_len, BAND_LAST, BAND_INTERIOR)

    def pool_group(gi):
        lo = gi * POOL_GW
        parts = []
        for sb in range(nsub):
            r0 = sb * POOL_SUB
            var_idx = first_var if sb == 0 else (last_var if sb == nsub - 1 else BAND_INTERIOR)
            pg = jnp.dot(band_ref[var_idx, gi], u_scr[r0:r0 + POOL_WIN, lo:lo + POOL_GW],
                         preferred_element_type=F32)
            parts.append(pg.astype(BF16))
        yp = jnp.dot(jnp.concatenate(parts, axis=0), pw_ref[gi], preferred_element_type=F32)
        yp_scr[:, lo:lo + POOL_GW] = yp * ps_ref[:, lo:lo + POOL_GW]

    row = lax.broadcasted_iota(jnp.int32, (2 * DK, DV), 0)
    even_row = ((row >> 5) & 1) == 0
    row_even = even_row.astype(F32)
    s_mask = (row_even.astype(BF16), (1.0 - row_even).astype(BF16))
    zeros_v = jnp.zeros((c, DV), BF16)
    order = [(ci, p) for ci in reversed(range(tb // c)) for p in range(PAIRS)]

    def expand(w):
        return jnp.concatenate([w * s_mask[0], w * s_mask[1]], axis=1)

    def early(j):
        ci, p = order[j]
        rows = slice(ci * c, (ci + 1) * c)
        kt = kt_ref[ci, p]
        q = proj_ref[rows, OFF_Q + p * 128:OFF_Q + (p + 1) * 128]
        s = jnp.dot(q, expand(kt), preferred_element_type=F32)
        pm = (s * m_ref[p]).astype(BF16)
        vp = proj_ref[rows, OFF_V + 2 * p * DV:OFF_V + (2 * p + 2) * DV]
        kv = jnp.dot((kt.astype(F32) * zbt_ref[p]).astype(BF16), vp, preferred_element_type=F32)
        return pm, jnp.where(even_row, kv[:, 0:DV], kv[:, DV:2 * DV])

    def values(j, pm_kvb):
        pm, kvb = pm_kvb
        ci, p = order[j]
        rows = slice(ci * c, (ci + 1) * c)
        lo = p * 128
        qf = proj_ref[rows, OFF_QF + lo:OFF_QF + lo + 128]
        qb = proj_ref[rows, OFF_QB + lo:OFF_QB + lo + 128]
        v0 = proj_ref[rows, OFF_V + 2 * lo:OFF_V + 2 * lo + DV]
        v1 = proj_ref[rows, OFF_V + 2 * lo + DV:OFF_V + 2 * lo + 2 * DV]
        cur_b = stb[p]
        stb[p] = cur_b * dcb_ref[p] + kvb
        rhs = jnp.concatenate(
            [jnp.concatenate([v0, zeros_v], axis=1),
             jnp.concatenate([zeros_v, v1], axis=1),
             expand(sf_ref[ci, p]),
             expand(cur_b.astype(BF16))], axis=0)
        o2 = jnp.dot(jnp.concatenate([pm, qf, qb], axis=1), rhs, preferred_element_type=F32)
        for hh in range(2):
            hl = (2 * p + hh) * DV
            o = o2[:, hh * DV:(hh + 1) * DV]
            mu = jnp.mean(o, axis=-1, keepdims=True)
            d = o - mu
            var = jnp.mean(d * d, axis=-1, keepdims=True)
            a = proj_ref[rows, OFF_A + hl:OFF_A + hl + DV].astype(F32)
            yr_scr[rows, hl:hl + DV] = a * (d * lax.rsqrt(var + EPS))
        if p == PAIRS - 1:
            sgp = proj_ref[rows, OFF_SGP:OFF_SGP + D_MODEL].astype(F32)
            m_scr[slot, rows, :] = (yr_scr[rows, :] + sgp * yp_scr[rows, :]).astype(BF16)

    n_wout = (tb // WOUT_ROWS) * (D_MODEL // WOUT_COLS)
    fillers = {0: lambda: pool_group(0), 1: lambda: pool_group(1), 2: lambda: pool_group(2),
               3: lambda: pool_group(3)}
    for k in range(1, n_wout - 1):
        fillers[2 + 2 * k] = functools.partial(wout_piece, k)
    assert max(fillers) < len(order)

    wout_piece(0)
    pending = {0: early(0), 1: early(1)}
    for j in range(len(order)):
        if j in fillers:
            fillers[j]()
        values(j, pending.pop(j))
        if j + 2 < len(order):
            pending[j + 2] = early(j + 2)
    wout_piece(n_wout - 1)


def _pool_bands():
    i = np.arange(POOL_SUB)[:, None]
    j = np.arange(POOL_WIN)[None, :] - HALO
    out = np.zeros((3, len(POOL_WINDOWS), POOL_SUB, POOL_WIN), np.float64)
    for gi, w in enumerate(POOL_WINDOWS):
        half = w // 2
        inside = ((j - i >= -half) & (j - i <= half - 1)).astype(np.float64)
        token = (j == i).astype(np.float64)
        counts = {
            BAND_INTERIOR: np.full((POOL_SUB, 1), float(w)),
            BAND_FIRST: (i + half) - np.maximum(i - half, 0),
            BAND_LAST: np.minimum(i + half, POOL_SUB) - (i - half),
        }
        for var, cnt in counts.items():
            out[var, gi] = inside / cnt - token
    return out.astype(np.float32)


def _mixer(proj, kt, sf, zbt, dcb, m_tab, pw, ps, wout, seq_len):
    t = proj.shape[0]
    tb = TB_MIX
    nb = t // tb
    cpb = tb // CHUNK
    hb = tb // HALO
    n_halo = t // HALO
    u_col = OFF_U // D_MODEL
    band = jnp.asarray(_pool_bands(), BF16)
    rev = lambda i: nb - 1 - jnp.minimum(i, nb - 1)
    rev_out = lambda i: nb - 1 - jnp.maximum(i - 1, 0)
    const2 = lambda i: (0, 0)
    const3 = lambda i: (0, 0, 0)
    state_blk = pl.BlockSpec((cpb, PAIRS, 2 * DK, DV), lambda i: (rev(i), 0, 0, 0))
    return pl.pallas_call(
        functools.partial(_mixer_kernel, seq_len),
        grid=(nb + 1,),
        in_specs=[
            pl.BlockSpec((tb, D_PROJ), lambda i: (rev(i), 0)),
            pl.BlockSpec((HALO, D_MODEL), lambda i: (jnp.maximum(rev(i) * hb - 1, 0), u_col)),
            pl.BlockSpec((HALO, D_MODEL), lambda i: (jnp.minimum((rev(i) + 1) * hb, n_halo - 1), u_col)),
            state_blk, state_blk,
            pl.BlockSpec((PAIRS, 2 * DK, CHUNK), const3),
            pl.BlockSpec((PAIRS, 2 * DK, DV), const3),
            pl.BlockSpec((PAIRS, CHUNK, 2 * CHUNK), const3),
            pl.BlockSpec(band.shape, lambda i: (0, 0, 0, 0)),
            pl.BlockSpec((len(POOL_WINDOWS), POOL_GW, POOL_GW), const3),
            pl.BlockSpec((1, D_MODEL), const2),
            pl.BlockSpec((D_MODEL, D_MODEL), const2),
        ],
        out_specs=pl.BlockSpec((tb, D_MODEL), lambda i: (rev_out(i), 0)),
        out_shape=jax.ShapeDtypeStruct((t, D_MODEL), F32),
        scratch_shapes=[
            pltpu.VMEM((tb, D_MODEL), F32),
            pltpu.VMEM((tb, D_MODEL), F32),
            pltpu.VMEM((tb - POOL_SUB + POOL_WIN, D_MODEL), BF16),
            pltpu.VMEM((2, tb, D_MODEL), BF16),
            pltpu.VMEM((PAIRS, 2 * DK, DV), F32),
        ],
        compiler_params=_params(),
    )(proj, proj, proj, kt, sf, zbt, dcb, m_tab, band, pw, ps, wout)


def _mlp_body(x_ref, z_ref, gmix_ref, gpre_ref, w1_ref, w2_ref, gpost_ref, out_ref, x1_rd, h_rd, x1_wr, h_wr):
    tb = x_ref.shape[0]
    n_ff = D_FF // D_MODEL
    rows_per = tb // (2 * n_ff)

    def norms(k):
        rows = slice(k * rows_per, (k + 1) * rows_per)
        x1 = x_ref[rows, :] + _rms(z_ref[rows, :], gmix_ref[...])
        x1_wr[rows, :] = x1
        h_wr[rows, :] = _rms(x1, gpre_ref[...]).astype(BF16)
        return _zero_like_row(x1[0:1, :])

    h = h_rd[...]
    acc = jnp.zeros((tb, D_MODEL), F32)
    for j in range(n_ff):
        lo = j * D_MODEL
        floor = jnp.maximum(norms(2 * j), norms(2 * j + 1))
        f = jnp.dot(h, w1_ref[:, lo:lo + D_MODEL], preferred_element_type=F32)
        f = jnp.square(jnp.maximum(f, floor)).astype(BF16)
        if j < n_ff - 1:
            acc = acc + jnp.dot(f, w2_ref[lo:lo + D_MODEL, :], preferred_element_type=F32)
    for rows in (slice(0, tb // 2), slice(tb // 2, tb)):
        y = acc[rows, :] + jnp.dot(f[rows, :], w2_ref[lo:lo + D_MODEL, :], preferred_element_type=F32)
        out_ref[rows, :] = x1_rd[rows, :] + _rms(y, gpost_ref[...])


def _mlp_kernel(x_ref, z_ref, gmix_ref, gpre_ref, w1_ref, w2_ref, gpost_ref, out_ref, x1_a, h_a, x1_b, h_b):
    step = pl.program_id(0)
    args = (x_ref, z_ref, gmix_ref, gpre_ref, w1_ref, w2_ref, gpost_ref, out_ref)

    @pl.when(step == 0)
    def _():
        x1_b[...] = jnp.zeros_like(x1_b)
        h_b[...] = jnp.zeros_like(h_b)

    @pl.when(lax.rem(step, 2) == 0)
    def _():
        _mlp_body(*args, x1_b, h_b, x1_a, h_a)

    @pl.when(lax.rem(step, 2) == 1)
    def _():
        _mlp_body(*args, x1_a, h_a, x1_b, h_b)


def _mlp(x, z, gmix, gpre, w1, w2, gpost):
    t = x.shape[0]
    tb = TB_MLP
    nb = t // tb
    const2 = lambda i: (0, 0)
    cur = lambda i: (jnp.minimum(i, nb - 1), 0)
    return pl.pallas_call(
        _mlp_kernel,
        grid=(nb + 1,),
        in_specs=[
            pl.BlockSpec((tb, D_MODEL), cur),
            pl.BlockSpec((tb, D_MODEL), cur),
            pl.BlockSpec((1, D_MODEL), const2),
            pl.BlockSpec((1, D_MODEL), const2),
            pl.BlockSpec((D_MODEL, D_FF), const2, pipeline_mode=pl.Buffered(1)),
            pl.BlockSpec((D_FF, D_MODEL), const2, pipeline_mode=pl.Buffered(1)),
            pl.BlockSpec((1, D_MODEL), const2),
        ],
        out_specs=pl.BlockSpec((tb, D_MODEL), lambda i: (jnp.maximum(i - 1, 0), 0)),
        out_shape=jax.ShapeDtypeStruct((t, D_MODEL), F32),
        scratch_shapes=[pltpu.VMEM((tb, D_MODEL), F32), pltpu.VMEM((tb, D_MODEL), BF16),
                        pltpu.VMEM((tb, D_MODEL), F32), pltpu.VMEM((tb, D_MODEL), BF16)],
        compiler_params=_params(),
    )(x, z, gmix, gpre, w1, w2, gpost)


def _qk_column_order():
    n = np.arange(D_QK)
    pair, l = n // 128, n % 128
    head = 2 * pair + (l // 32) % 2
    return head * DK + (l // 64) * (DK // 2) + l % 32


def _rope_tables(seq_len):
    half = DK // 2
    inv = ROPE_BASE ** (-jnp.arange(half, dtype=F32) / half)
    ang = jnp.arange(seq_len, dtype=F32)[:, None] * inv[None, :]
    cos, sin = jnp.cos(ang), jnp.sin(ang)
    return jnp.tile(cos, (1, 4)), jnp.concatenate([-sin, -sin, sin, sin], axis=1)


def _trunk(x, layers, tabs, cos_t, sin_t):
    batch, seq_len, _ = x.shape
    xt = x.reshape(batch * seq_len, D_MODEL)
    m_tab, xif, xib, zft, zbt, dcf, dcb = tabs
    for l, lw in enumerate(layers):
        proj, kt, sf = _inproj(xt, lw["g_mix_pre"], lw["w_qk"], lw["w_rest"], cos_t, sin_t,
                               xif[l], xib[l], zft[l], dcf[l], lw["gn"], seq_len)
        z = _mixer(proj, kt, sf, zbt[l], dcb[l], m_tab[l], lw["pool_w"], lw["pool_scale"], lw["w_out"], seq_len)
        xt = _mlp(xt, z, lw["g_mix_post"], lw["g_mlp_pre"], lw["w_mlp1"], lw["w_mlp2"], lw["g_mlp_post"])
    return xt.reshape(batch, seq_len, D_MODEL)


def kernel(x_prompt, x_sample, norm_mix_pre, norm_mix_post, w_in, ret_decay_fwd, ret_decay_bwd, ret_gn,
           pool_w, pool_scale, w_out, norm_mlp_pre, norm_mlp_post, w_mlp1, w_mlp2):
    order = _qk_column_order()
    qk_cols = np.concatenate([order, D_QK + order])
    w_qk = w_in[:, :, :2 * D_QK][:, :, qk_cols].astype(BF16)
    w_rest = w_in[:, :, 2 * D_QK:].astype(BF16)
    layers = []
    for l in range(DEPTH):
        layers.append(dict(
            g_mix_pre=norm_mix_pre[l][None], g_mix_post=norm_mix_post[l][None],
            w_qk=w_qk[l], w_rest=w_rest[l], gn=ret_gn[l][None], pool_w=pool_w[l].astype(BF16),
            pool_scale=pool_scale[l][None], w_out=w_out[l].astype(BF16),
            g_mlp_pre=norm_mlp_pre[l][None], g_mlp_post=norm_mlp_post[l][None],
            w_mlp1=w_mlp1[l].astype(BF16), w_mlp2=w_mlp2[l].astype(BF16)))
    dec = jnp.concatenate([ret_decay_fwd, ret_decay_bwd], axis=1).astype(F32)
    tabs = _decay_tables(dec)
    cos_t, sin_t = _rope_tables(max(x_prompt.shape[1], x_sample.shape[1]))
    y_prompt = _trunk(x_prompt, layers, tabs, cos_t, sin_t)
    y_sample = _trunk(x_sample, layers, tabs, cos_t, sin_t)
    return (y_prompt, y_sample)
```

```python
import functools

import numpy as np
import jax
import jax.numpy as jnp
from jax import lax
from jax.experimental import pallas as pl
from jax.experimental.pallas import tpu as pltpu

D_MODEL = 1024
DEPTH = 4
HEADS = 8
DK = 64
DV = 128
PAIRS = HEADS // 2
D_QK = HEADS * DK
CHUNK = 128
ROPE_BASE = 10000.0
POOL_WINDOWS = (2, 4, 8, 16)
POOL_GW = 256
D_FF = 4 * D_MODEL
EPS = 1e-6
QK_SCALE = DK ** -0.5

OFF_U, OFF_V, OFF_A, OFF_SGP, OFF_Q, OFF_QF, OFF_QB = 0, 1024, 2048, 3072, 4096, 4608, 5120
D_PROJ = 5632
W_V, W_G, W_U, W_GR, W_GP = 0, 1024, 2048, 3072, 4096
HALO = 16
POOL_SUB = 128
POOL_WIN = 256
BAND_INTERIOR, BAND_FIRST, BAND_LAST = 0, 1, 2

TB_IN = 512
TB_MIX = 512
WOUT_ROWS, WOUT_COLS = 256, 256
TB_MLP = 512
MLP_SPLIT = 1
VMEM_LIMIT_BYTES = 56 * 1024 * 1024

F32 = jnp.float32
BF16 = jnp.bfloat16


def _rms(x, g):
    return x * lax.rsqrt(jnp.mean(x * x, axis=-1, keepdims=True) + EPS) * g


def _zero_like_row(row):
    bits = lax.bitcast_convert_type(row, jnp.uint32)
    return lax.bitcast_convert_type((bits >> 16) >> 16, F32)


def _log_sigmoid(d):
    return jnp.minimum(d, 0.0) - jnp.log1p(jnp.exp(-jnp.abs(d)))


def _params():
    return pltpu.CompilerParams(dimension_semantics=("arbitrary",), vmem_limit_bytes=VMEM_LIMIT_BYTES)


def _tables_kernel(dec_ref, m_ref, xif_ref, xib_ref, zft_ref, zbt_ref, dcf_ref, dcb_ref):
    l = pl.program_id(0)
    c = CHUNK
    df = [dec_ref[l, h] for h in range(HEADS)]
    db = [dec_ref[l, HEADS + h] for h in range(HEADS)]

    ii = lax.broadcasted_iota(jnp.int32, (c, c), 0)
    jj = lax.broadcasted_iota(jnp.int32, (c, c), 1)
    dist = jnp.abs(ii - jj).astype(F32)
    for h in range(HEADS):
        dsel = jnp.where(ii >= jj, jnp.full((c, c), df[h], F32), jnp.full((c, c), db[h], F32))
        m_ref[0, h // 2, :, (h % 2) * c:(h % 2 + 1) * c] = jnp.exp(dist * _log_sigmoid(dsel))

    rows = xif_ref.shape[1]
    lane = lax.broadcasted_iota(jnp.int32, (rows, D_QK), 1)
    head = 2 * (lane >> 7) + ((lane >> 5) & 1)
    idx = (lax.broadcasted_iota(jnp.int32, (rows, D_QK), 0) & (c - 1)).astype(F32)
    sel_f = jnp.full((rows, D_QK), df[0], F32)
    sel_b = jnp.full((rows, D_QK), db[0], F32)
    for h in range(1, HEADS):
        sel_f = jnp.where(head == h, df[h], sel_f)
        sel_b = jnp.where(head == h, db[h], sel_b)
    xif_ref[0] = jnp.exp((idx + 1.0) * _log_sigmoid(sel_f))
    xib_ref[0] = jnp.exp((c - idx) * _log_sigmoid(sel_b))

    rl = lax.broadcasted_iota(jnp.int32, (2 * DK, c), 0)
    odd = ((rl >> 5) & 1) == 1
    jc = lax.broadcasted_iota(jnp.int32, (2 * DK, c), 1).astype(F32)
    for p in range(PAIRS):
        lg_f = _log_sigmoid(jnp.where(odd, df[2 * p + 1], df[2 * p]))
        lg_b = _log_sigmoid(jnp.where(odd, db[2 * p + 1], db[2 * p]))
        zft_ref[0, p] = jnp.exp((c - 1.0 - jc) * lg_f)
        zbt_ref[0, p] = jnp.exp(jc * lg_b)
        dcf_ref[0, p] = jnp.exp(float(c) * lg_f)
        dcb_ref[0, p] = jnp.exp(float(c) * lg_b)


def _decay_tables(dec):
    c = CHUNK
    pair_tab = jax.ShapeDtypeStruct((DEPTH, PAIRS, 2 * DK, c), F32)
    pair_spec = pl.BlockSpec((1, PAIRS, 2 * DK, c), lambda l: (l, 0, 0, 0))
    lane_tab = jax.ShapeDtypeStruct((DEPTH, TB_IN, D_QK), F32)
    lane_spec = pl.BlockSpec((1, TB_IN, D_QK), lambda l: (l, 0, 0))
    return pl.pallas_call(
        _tables_kernel,
        grid=(DEPTH,),
        in_specs=[pl.BlockSpec(memory_space=pltpu.SMEM)],
        out_specs=(pl.BlockSpec((1, PAIRS, c, 2 * c), lambda l: (l, 0, 0, 0)),
                   lane_spec, lane_spec, pair_spec, pair_spec, pair_spec, pair_spec),
        out_shape=(jax.ShapeDtypeStruct((DEPTH, PAIRS, c, 2 * c), F32),
                   lane_tab, lane_tab, pair_tab, pair_tab, pair_tab, pair_tab),
        compiler_params=_params(),
    )(dec)


def _inproj_body(x_ref, g_ref, wqk_ref, w_ref, cos_ref, sin_ref, xif_ref, xib_ref,
                 zft_ref, dcf_ref, gn_ref, proj_ref, kt_ref, sf_ref, k_scr, stf, h_rd, h_wr):
    tb = x_ref.shape[0]
    n_groups = 4
    rows_per = tb // n_groups

    def norm_group(k):
        rows = slice(k * rows_per, (k + 1) * rows_per)
        xn = _rms(x_ref[rows, :], g_ref[...])
        h_wr[rows, :] = xn.astype(BF16)
        return _zero_like_row(xn[0:1, :])

    h = h_rd[...]
    cos = cos_ref[...] + norm_group(0)[:, 0:128]
    sin = sin_ref[...] + norm_group(1)[:, 0:128]

    qk = jnp.dot(h, wqk_ref[...], preferred_element_type=F32)
    for p in range(PAIRS):
        lo = p * 128
        qp = qk[:, lo:lo + 128]
        qr = (qp * cos + pltpu.roll(qp, 64, 1) * sin) * QK_SCALE
        proj_ref[:, OFF_Q + lo:OFF_Q + lo + 128] = qr.astype(BF16)
        proj_ref[:, OFF_QF + lo:OFF_QF + lo + 128] = (qr * xif_ref[:, lo:lo + 128]).astype(BF16)
        proj_ref[:, OFF_QB + lo:OFF_QB + lo + 128] = (qr * xib_ref[:, lo:lo + 128]).astype(BF16)
        kp = qk[:, D_QK + lo:D_QK + lo + 128]
        kr = kp * cos + pltpu.roll(kp, 64, 1) * sin
        k_scr[:, lo:lo + 128] = kr

    def section(w_off):
        return jnp.dot(h, w_ref[:, w_off:w_off + D_MODEL], preferred_element_type=F32)

    v = section(W_V).astype(BF16)
    proj_ref[:, OFF_V:OFF_V + D_MODEL] = v

    rl = lax.broadcasted_iota(jnp.int32, (2 * DK, DV), 0)
    even_row = ((rl >> 5) & 1) == 0

    def summaries(c, zft):
        r0 = c * CHUNK
        for p in range(PAIRS):
            kt = k_scr[r0:r0 + CHUNK, p * 128:(p + 1) * 128].T
            kt_ref[c, p] = kt.astype(BF16)
            vp = v[r0:r0 + CHUNK, 2 * p * DV:(2 * p + 2) * DV]
            kv = jnp.dot((kt * zft[p]).astype(BF16), vp, preferred_element_type=F32)
            cur = stf[p]
            sf_ref[c, p] = cur.astype(BF16)
            stf[p] = cur * dcf_ref[p] + jnp.where(even_row, kv[:, 0:DV], kv[:, DV:2 * DV])

    proj_ref[:, OFF_U:OFF_U + D_MODEL] = section(W_U).astype(BF16)
    g = section(W_G)
    gain = gn_ref[...] + norm_group(2)
    a_gate = g * jax.nn.sigmoid(g) * jax.nn.sigmoid(section(W_GR)) * gain
    proj_ref[:, OFF_A:OFF_A + D_MODEL] = a_gate.astype(BF16)
    proj_ref[:, OFF_SGP:OFF_SGP + D_MODEL] = jax.nn.sigmoid(section(W_GP)).astype(BF16)
    tie = norm_group(3)[:, 0:CHUNK]
    zft = [zft_ref[p] + tie for p in range(PAIRS)]
    for c in range(tb // CHUNK):
        summaries(c, zft)


def _inproj_kernel(seq_len, x_ref, g_ref, wqk_ref, w_ref, cos_ref, sin_ref, xif_ref, xib_ref,
                   zft_ref, dcf_ref, gn_ref, proj_ref, kt_ref, sf_ref, k_scr, stf, h_a, h_b):
    step = pl.program_id(0)
    tb = x_ref.shape[0]
    args = (x_ref, g_ref, wqk_ref, w_ref, cos_ref, sin_ref, xif_ref, xib_ref, zft_ref, dcf_ref, gn_ref,
            proj_ref, kt_ref, sf_ref, k_scr, stf)

    @pl.when(step == 0)
    def _():
        h_b[...] = jnp.zeros_like(h_b)

    @pl.when(lax.rem(jnp.maximum(step - 1, 0) * tb, seq_len) == 0)
    def _():
        stf[...] = jnp.zeros_like(stf)

    @pl.when(lax.rem(step, 2) == 0)
    def _():
        _inproj_body(*args, h_b, h_a)

    @pl.when(lax.rem(step, 2) == 1)
    def _():
        _inproj_body(*args, h_a, h_b)


def _inproj(x, layer, w, tabs, cos_t, sin_t, seq_len):
    t = x.shape[0]
    tb = TB_IN
    nb = t // tb
    nb_seq = seq_len // tb
    cpb = tb // CHUNK
    xif, xib, zft, dcf = tabs
    prv = lambda i: jnp.maximum(i - 1, 0)
    row_l = pl.BlockSpec((None, 1, D_MODEL), lambda i: (layer, 0, 0))
    pair_l = pl.BlockSpec((None, PAIRS, 2 * DK, CHUNK), lambda i: (layer, 0, 0, 0))
    lane_l = pl.BlockSpec((None, tb, D_QK), lambda i: (layer, 0, 0))
    rope_blk = pl.BlockSpec((tb, 128), lambda i: (prv(i) % nb_seq, 0))
    state_blk = pl.BlockSpec((cpb, PAIRS, 2 * DK, DV), lambda i: (prv(i), 0, 0, 0))
    return pl.pallas_call(
        functools.partial(_inproj_kernel, seq_len),
        grid=(nb + 1,),
        in_specs=[
            pl.BlockSpec((tb, D_MODEL), lambda i: (jnp.minimum(i, nb - 1), 0)),
            row_l,
            pl.BlockSpec((None,) + w["w_qk"].shape[1:], lambda i: (layer, 0, 0), pipeline_mode=pl.Buffered(1)),
            pl.BlockSpec((None,) + w["w_rest"].shape[1:], lambda i: (layer, 0, 0), pipeline_mode=pl.Buffered(1)),
            rope_blk, rope_blk,
            lane_l, lane_l,
            pair_l, pair_l,
            row_l,
        ],
        out_specs=(pl.BlockSpec((tb, D_PROJ), lambda i: (prv(i), 0)), state_blk, state_blk),
        out_shape=(
            jax.ShapeDtypeStruct((t, D_PROJ), BF16),
            jax.ShapeDtypeStruct((t // CHUNK, PAIRS, 2 * DK, CHUNK), BF16),
            jax.ShapeDtypeStruct((t // CHUNK, PAIRS, 2 * DK, DV), BF16),
        ),
        scratch_shapes=[pltpu.VMEM((tb, D_QK), F32), pltpu.VMEM((PAIRS, 2 * DK, DV), F32),
                        pltpu.VMEM((tb, D_MODEL), BF16), pltpu.VMEM((tb, D_MODEL), BF16)],
        compiler_params=_params(),
    )(x, w["g_mix_pre"], w["w_qk"], w["w_rest"], cos_t, sin_t, xif, xib, zft, dcf, w["gn"])


def _mixer_kernel(seq_len, proj_ref, uprev_ref, unext_ref, kt_ref, sf_ref, zbt_ref, dcb_ref, m_ref, band_ref,
                  pw_ref, ps_ref, wout_ref, out_ref, yr_scr, yp_scr, u_scr, m_scr, stb):
    tb = proj_ref.shape[0]
    c = CHUNK
    step = pl.program_id(0)
    nblk = pl.num_programs(0) - 1
    blk = nblk - 1 - jnp.minimum(step, nblk - 1)
    s0 = lax.rem(blk * tb, seq_len)
    slot = lax.rem(step, 2)

    @pl.when(step == 0)
    def _():
        m_scr[1] = jnp.zeros(m_scr.shape[1:], BF16)

    @pl.when(s0 + tb == seq_len)
    def _():
        stb[...] = jnp.zeros_like(stb)

    def wout_piece(idx):
        rh, nt = divmod(idx, D_MODEL // WOUT_COLS)
        rows = slice(rh * WOUT_ROWS, (rh + 1) * WOUT_ROWS)
        cols = slice(nt * WOUT_COLS, (nt + 1) * WOUT_COLS)
        out_ref[rows, cols] = jnp.dot(m_scr[1 - slot, rows, :], wout_ref[:, cols], preferred_element_type=F32)

    keep_prev = jnp.where(s0 > 0, 1.0, 0.0)
    keep_next = jnp.where(s0 + tb < seq_len, 1.0, 0.0)
    u_scr[0:HALO, :] = (uprev_ref[...].astype(F32) * keep_prev).astype(BF16)
    u_scr[HALO:HALO + tb, :] = proj_ref[:, OFF_U:OFF_U + D_MODEL]
    u_scr[HALO + tb:2 * HALO + tb, :] = (unext_ref[...].astype(F32) * keep_next).astype(BF16)
    u_scr[2 * HALO + tb:, :] = jnp.zeros((u_scr.shape[0] - 2 * HALO - tb, D_MODEL), BF16)
    nsub = tb // POOL_SUB
    first_var = jnp.where(s0 == 0, BAND_FIRST, BAND_INTERIOR)
    last_var = jnp.where(s0 + tb == seq_len, BAND_LAST, BAND_INTERIOR)

    def pool_group(gi):
        lo = gi * POOL_GW
        parts = []
        for sb in range(nsub):
            r0 = sb * POOL_SUB
            var_idx = first_var if sb == 0 else (last_var if sb == nsub - 1 else BAND_INTERIOR)
            pg = jnp.dot(band_ref[var_idx, gi], u_scr[r0:r0 + POOL_WIN, lo:lo + POOL_GW],
                         preferred_element_type=F32)
            parts.append(pg.astype(BF16))
        yp = jnp.dot(jnp.concatenate(parts, axis=0), pw_ref[gi], preferred_element_type=F32)
        yp_scr[:, lo:lo + POOL_GW] = yp * ps_ref[:, lo:lo + POOL_GW]

    row = lax.broadcasted_iota(jnp.int32, (2 * DK, DV), 0)
    even_row = ((row >> 5) & 1) == 0
    row_even = even_row.astype(F32)
    s_mask = (row_even.astype(BF16), (1.0 - row_even).astype(BF16))
    zeros_v = jnp.zeros((c, DV), BF16)
    order = [(ci, p) for ci in reversed(range(tb // c)) for p in range(PAIRS)]

    def expand(w):
        return jnp.concatenate([w * s_mask[0], w * s_mask[1]], axis=1)

    def early(j):
        ci, p = order[j]
        rows = slice(ci * c, (ci + 1) * c)
        kt = kt_ref[ci, p]
        q = proj_ref[rows, OFF_Q + p * 128:OFF_Q + (p + 1) * 128]
        s = jnp.dot(q, expand(kt), preferred_element_type=F32)
        pm = (s * m_ref[p]).astype(BF16)
        vp = proj_ref[rows, OFF_V + 2 * p * DV:OFF_V + (2 * p + 2) * DV]
        kv = jnp.dot((kt.astype(F32) * zbt_ref[p]).astype(BF16), vp, preferred_element_type=F32)
        return pm, jnp.where(even_row, kv[:, 0:DV], kv[:, DV:2 * DV])

    def values(j, pm_kvb):
        pm, kvb = pm_kvb
        ci, p = order[j]
        rows = slice(ci * c, (ci + 1) * c)
        lo = p * 128
        qf = proj_ref[rows, OFF_QF + lo:OFF_QF + lo + 128]
        qb = proj_ref[rows, OFF_QB + lo:OFF_QB + lo + 128]
        v0 = proj_ref[rows, OFF_V + 2 * lo:OFF_V + 2 * lo + DV]
        v1 = proj_ref[rows, OFF_V + 2 * lo + DV:OFF_V + 2 * lo + 2 * DV]
        cur_b = stb[p]
        stb[p] = cur_b * dcb_ref[p] + kvb
        rhs = jnp.concatenate(
            [jnp.concatenate([v0, zeros_v], axis=1),
             jnp.concatenate([zeros_v, v1], axis=1),
             expand(sf_ref[ci, p]),
             expand(cur_b.astype(BF16))], axis=0)
        o2 = jnp.dot(jnp.concatenate([pm, qf, qb], axis=1), rhs, preferred_element_type=F32)
        for hh in range(2):
            hl = (2 * p + hh) * DV
            o = o2[:, hh * DV:(hh + 1) * DV]
            mu = jnp.mean(o, axis=-1, keepdims=True)
            d = o - mu
            var = jnp.mean(d * d, axis=-1, keepdims=True)
            a = proj_ref[rows, OFF_A + hl:OFF_A + hl + DV].astype(F32)
            yr_scr[rows, hl:hl + DV] = a * (d * lax.rsqrt(var + EPS))
        if p == PAIRS - 1:
            sgp = proj_ref[rows, OFF_SGP:OFF_SGP + D_MODEL].astype(F32)
            m_scr[slot, rows, :] = (yr_scr[rows, :] + sgp * yp_scr[rows, :]).astype(BF16)

    n_wout = (tb // WOUT_ROWS) * (D_MODEL // WOUT_COLS)
    fillers = {0: lambda: pool_group(0), 1: lambda: pool_group(1), 2: lambda: pool_group(2),
               3: lambda: pool_group(3)}
    for k in range(1, n_wout - 1):
        fillers[2 + 2 * k] = functools.partial(wout_piece, k)
    assert max(fillers) < len(order)

    wout_piece(0)
    pending = {0: early(0), 1: early(1)}
    for j in range(len(order)):
        if j in fillers:
            fillers[j]()
        values(j, pending.pop(j))
        if j + 2 < len(order):
            pending[j + 2] = early(j + 2)
    wout_piece(n_wout - 1)


def _pool_bands():
    i = np.arange(POOL_SUB)[:, None]
    j = np.arange(POOL_WIN)[None, :] - HALO
    out = np.zeros((3, len(POOL_WINDOWS), POOL_SUB, POOL_WIN), np.float64)
    for gi, w in enumerate(POOL_WINDOWS):
        half = w // 2
        inside = ((j - i >= -half) & (j - i <= half - 1)).astype(np.float64)
        token = (j == i).astype(np.float64)
        counts = {
            BAND_INTERIOR: np.full((POOL_SUB, 1), float(w)),
            BAND_FIRST: (i + half) - np.maximum(i - half, 0),
            BAND_LAST: np.minimum(i + half, POOL_SUB) - (i - half),
        }
        for var, cnt in counts.items():
            out[var, gi] = inside / cnt - token
    return out.astype(np.float32)


def _mixer(proj, kt, sf, layer, w, tabs, seq_len):
    t = proj.shape[0]
    tb = TB_MIX
    nb = t // tb
    cpb = tb // CHUNK
    hb = tb // HALO
    n_halo = t // HALO
    u_col = OFF_U // D_MODEL
    zbt, dcb, m_tab = tabs
    band = jnp.asarray(_pool_bands(), BF16)
    rev = lambda i: nb - 1 - jnp.minimum(i, nb - 1)
    rev_out = lambda i: nb - 1 - jnp.maximum(i - 1, 0)
    lay3 = lambda i: (layer, 0, 0)
    lay4 = lambda i: (layer, 0, 0, 0)
    state_blk = pl.BlockSpec((cpb, PAIRS, 2 * DK, DV), lambda i: (rev(i), 0, 0, 0))
    return pl.pallas_call(
        functools.partial(_mixer_kernel, seq_len),
        grid=(nb + 1,),
        in_specs=[
            pl.BlockSpec((tb, D_PROJ), lambda i: (rev(i), 0)),
            pl.BlockSpec((HALO, D_MODEL), lambda i: (jnp.maximum(rev(i) * hb - 1, 0), u_col)),
            pl.BlockSpec((HALO, D_MODEL), lambda i: (jnp.minimum((rev(i) + 1) * hb, n_halo - 1), u_col)),
            state_blk, state_blk,
            pl.BlockSpec((None, PAIRS, 2 * DK, CHUNK), lay4),
            pl.BlockSpec((None, PAIRS, 2 * DK, DV), lay4),
            pl.BlockSpec((None, PAIRS, CHUNK, 2 * CHUNK), lay4),
            pl.BlockSpec(band.shape, lambda i: (0, 0, 0, 0)),
            pl.BlockSpec((None, len(POOL_WINDOWS), POOL_GW, POOL_GW), lay4),
            pl.BlockSpec((None, 1, D_MODEL), lay3),
            pl.BlockSpec((None, D_MODEL, D_MODEL), lay3),
        ],
        out_specs=pl.BlockSpec((tb, D_MODEL), lambda i: (rev_out(i), 0)),
        out_shape=jax.ShapeDtypeStruct((t, D_MODEL), F32),
        scratch_shapes=[
            pltpu.VMEM((tb, D_MODEL), F32),
            pltpu.VMEM((tb, D_MODEL), F32),
            pltpu.VMEM((tb - POOL_SUB + POOL_WIN, D_MODEL), BF16),
            pltpu.VMEM((2, tb, D_MODEL), BF16),
            pltpu.VMEM((PAIRS, 2 * DK, DV), F32),
        ],
        compiler_params=_params(),
    )(proj, proj, proj, kt, sf, zbt, dcb, m_tab, band, w["pool_w"], w["pool_scale"], w["w_out"])


def _mlp_body(x_ref, z_ref, gmix_ref, gpre_ref, w1_ref, w2_ref, gpost_ref, out_ref, x1_rd, h_rd, x1_wr, h_wr):
    tb = x_ref.shape[0]
    n_ff = D_FF // D_MODEL
    rows_per = tb // (2 * n_ff)

    def norms(k):
        rows = slice(k * rows_per, (k + 1) * rows_per)
        x1 = x_ref[rows, :] + _rms(z_ref[rows, :], gmix_ref[...])
        x1_wr[rows, :] = x1
        h_wr[rows, :] = _rms(x1, gpre_ref[...]).astype(BF16)
        return _zero_like_row(x1[0:1, :])

    h = h_rd[...]
    acc = jnp.zeros((tb, D_MODEL), F32)
    for j in range(n_ff):
        lo = j * D_MODEL
        floor = jnp.maximum(norms(2 * j), norms(2 * j + 1))
        f = jnp.dot(h, w1_ref[:, lo:lo + D_MODEL], preferred_element_type=F32)
        f = jnp.square(jnp.maximum(f, floor)).astype(BF16)
        acc = acc + jnp.dot(f, w2_ref[lo:lo + D_MODEL, :], preferred_element_type=F32)
    out_ref[...] = x1_rd[...] + _rms(acc, gpost_ref[...])


def _mlp_kernel(x_ref, z_ref, gmix_ref, gpre_ref, w1_ref, w2_ref, gpost_ref, out_ref, x1_a, h_a, x1_b, h_b):
    step = pl.program_id(0)
    args = (x_ref, z_ref, gmix_ref, gpre_ref, w1_ref, w2_ref, gpost_ref, out_ref)

    @pl.when(step == 0)
    def _():
        x1_b[...] = jnp.zeros_like(x1_b)
        h_b[...] = jnp.zeros_like(h_b)

    @pl.when(lax.rem(step, 2) == 0)
    def _():
        _mlp_body(*args, x1_b, h_b, x1_a, h_a)

    @pl.when(lax.rem(step, 2) == 1)
    def _():
        _mlp_body(*args, x1_a, h_a, x1_b, h_b)


def _mlp(x, z, layer, w):
    t = x.shape[0]
    tb = TB_MLP
    nb = t // tb
    lay3 = lambda i: (layer, 0, 0)
    row_l = pl.BlockSpec((None, 1, D_MODEL), lay3)
    cur = lambda i: (jnp.minimum(i, nb - 1), 0)
    return pl.pallas_call(
        _mlp_kernel,
        grid=(nb + 1,),
        in_specs=[
            pl.BlockSpec((tb, D_MODEL), cur),
            pl.BlockSpec((tb, D_MODEL), cur),
            row_l, row_l,
            pl.BlockSpec((None, D_MODEL, D_FF), lay3, pipeline_mode=pl.Buffered(1)),
            pl.BlockSpec((None, D_FF, D_MODEL), lay3, pipeline_mode=pl.Buffered(1)),
            row_l,
        ],
        out_specs=pl.BlockSpec((tb, D_MODEL), lambda i: (jnp.maximum(i - 1, 0), 0)),
        out_shape=jax.ShapeDtypeStruct((t, D_MODEL), F32),
        scratch_shapes=[pltpu.VMEM((tb, D_MODEL), F32), pltpu.VMEM((tb, D_MODEL), BF16),
                        pltpu.VMEM((tb, D_MODEL), F32), pltpu.VMEM((tb, D_MODEL), BF16)],
        compiler_params=_params(),
    )(x, z, w["g_mix_post"], w["g_mlp_pre"], w["w_mlp1"], w["w_mlp2"], w["g_mlp_post"])


def _qk_column_order():
    n = np.arange(D_QK)
    pair, l = n // 128, n % 128
    head = 2 * pair + (l // 32) % 2
    return head * DK + (l // 64) * (DK // 2) + l % 32


def _rope_tables(seq_len):
    half = DK // 2
    inv = ROPE_BASE ** (-jnp.arange(half, dtype=F32) / half)
    ang = jnp.arange(seq_len, dtype=F32)[:, None] * inv[None, :]
    cos, sin = jnp.cos(ang), jnp.sin(ang)
    return jnp.tile(cos, (1, 4)), jnp.concatenate([-sin, -sin, sin, sin], axis=1)


def _trunk(x, w, tabs, cos_t, sin_t):
    batch, seq_len, _ = x.shape
    xt = x.reshape(batch * seq_len, D_MODEL)
    m_tab, xif, xib, zft, zbt, dcf, dcb = tabs
    for layer in range(DEPTH):
        proj, kt, sf = _inproj(xt, layer, w, (xif, xib, zft, dcf), cos_t, sin_t, seq_len)
        z = _mixer(proj, kt, sf, layer, w, (zbt, dcb, m_tab), seq_len)
        xt = _mlp(xt, z, layer, w)
    return xt.reshape(batch, seq_len, D_MODEL)


def kernel(x_prompt, x_sample, norm_mix_pre, norm_mix_post, w_in, ret_decay_fwd, ret_decay_bwd, ret_gn,
           pool_w, pool_scale, w_out, norm_mlp_pre, norm_mlp_post, w_mlp1, w_mlp2):
    order = _qk_column_order()
    qk_cols = np.concatenate([order, D_QK + order])
    w = dict(
        w_qk=w_in[:, :, :2 * D_QK][:, :, qk_cols].astype(BF16), w_rest=w_in[:, :, 2 * D_QK:].astype(BF16),
        pool_w=pool_w.astype(BF16), w_out=w_out.astype(BF16),
        w_mlp1=w_mlp1.astype(BF16), w_mlp2=w_mlp2.astype(BF16),
        g_mix_pre=norm_mix_pre[:, None, :], g_mix_post=norm_mix_post[:, None, :], gn=ret_gn[:, None, :],
        pool_scale=pool_scale[:, None, :], g_mlp_pre=norm_mlp_pre[:, None, :], g_mlp_post=norm_mlp_post[:, None, :])
    dec = jnp.concatenate([ret_decay_fwd, ret_decay_bwd], axis=1).astype(F32)
    tabs = _decay_tables(dec)
    cos_t, sin_t = _rope_tables(max(x_prompt.shape[1], x_sample.shape[1]))
    y_prompt = _trunk(x_prompt, w, tabs, cos_t, sin_t)
    y_sample = _trunk(x_sample, w, tabs, cos_t, sin_t)
    return (y_prompt, y_sample)
```

```python
import functools

import numpy as np
import jax
import jax.numpy as jnp
from jax import lax
from jax.experimental import pallas as pl
from jax.experimental.pallas import tpu as pltpu

D_MODEL = 1024
DEPTH = 4
HEADS = 8
DK = 64
DV = 128
PAIRS = HEADS // 2
D_QK = HEADS * DK
CHUNK = 128
ROPE_BASE = 10000.0
POOL_WINDOWS = (2, 4, 8, 16)
POOL_GW = 256
D_FF = 4 * D_MODEL
EPS = 1e-6
QK_SCALE = DK ** -0.5

OFF_U, OFF_V, OFF_A, OFF_SGP, OFF_Q, OFF_QF, OFF_QB = 0, 1024, 2048, 3072, 4096, 4608, 5120
D_PROJ = 5632
W_V, W_G, W_U, W_GR, W_GP = 0, 1024, 2048, 3072, 4096
HALO = 16
POOL_SUB = 128
POOL_WIN = 256
BAND_INTERIOR, BAND_FIRST, BAND_LAST = 0, 1, 2

TB_IN = 512
TB_MIX = 512
WOUT_ROWS, WOUT_COLS = 256, 256
TB_MLP = 512
MLP_SPLIT = 1
VMEM_LIMIT_BYTES = 56 * 1024 * 1024

F32 = jnp.float32
BF16 = jnp.bfloat16


def _rms(x, g):
    return x * lax.rsqrt(jnp.mean(x * x, axis=-1, keepdims=True) + EPS) * g


def _zero_like_row(row):
    bits = lax.bitcast_convert_type(row, jnp.uint32)
    return lax.bitcast_convert_type((bits >> 16) >> 16, F32)


def _log_sigmoid(d):
    return jnp.minimum(d, 0.0) - jnp.log1p(jnp.exp(-jnp.abs(d)))


def _params():
    return pltpu.CompilerParams(dimension_semantics=("arbitrary",), vmem_limit_bytes=VMEM_LIMIT_BYTES)


def _tables_kernel(dec_ref, m_ref, xif_ref, xib_ref, zft_ref, zbt_ref, dcf_ref, dcb_ref):
    l = pl.program_id(0)
    c = CHUNK
    df = [dec_ref[l, h] for h in range(HEADS)]
    db = [dec_ref[l, HEADS + h] for h in range(HEADS)]

    ii = lax.broadcasted_iota(jnp.int32, (c, c), 0)
    jj = lax.broadcasted_iota(jnp.int32, (c, c), 1)
    dist = jnp.abs(ii - jj).astype(F32)
    for h in range(HEADS):
        dsel = jnp.where(ii >= jj, jnp.full((c, c), df[h], F32), jnp.full((c, c), db[h], F32))
        m_ref[0, h // 2, :, (h % 2) * c:(h % 2 + 1) * c] = jnp.exp(dist * _log_sigmoid(dsel))

    rows = xif_ref.shape[1]
    lane = lax.broadcasted_iota(jnp.int32, (rows, D_QK), 1)
    head = 2 * (lane >> 7) + ((lane >> 5) & 1)
    idx = (lax.broadcasted_iota(jnp.int32, (rows, D_QK), 0) & (c - 1)).astype(F32)
    sel_f = jnp.full((rows, D_QK), df[0], F32)
    sel_b = jnp.full((rows, D_QK), db[0], F32)
    for h in range(1, HEADS):
        sel_f = jnp.where(head == h, df[h], sel_f)
        sel_b = jnp.where(head == h, db[h], sel_b)
    xif_ref[0] = jnp.exp((idx + 1.0) * _log_sigmoid(sel_f))
    xib_ref[0] = jnp.exp((c - idx) * _log_sigmoid(sel_b))

    rl = lax.broadcasted_iota(jnp.int32, (2 * DK, c), 0)
    odd = ((rl >> 5) & 1) == 1
    jc = lax.broadcasted_iota(jnp.int32, (2 * DK, c), 1).astype(F32)
    for p in range(PAIRS):
        lg_f = _log_sigmoid(jnp.where(odd, df[2 * p + 1], df[2 * p]))
        lg_b = _log_sigmoid(jnp.where(odd, db[2 * p + 1], db[2 * p]))
        zft_ref[0, p] = jnp.exp((c - 1.0 - jc) * lg_f)
        zbt_ref[0, p] = jnp.exp(jc * lg_b)
        dcf_ref[0, p] = jnp.exp(float(c) * lg_f)
        dcb_ref[0, p] = jnp.exp(float(c) * lg_b)


def _decay_tables(dec):
    c = CHUNK
    pair_tab = jax.ShapeDtypeStruct((DEPTH, PAIRS, 2 * DK, c), F32)
    pair_spec = pl.BlockSpec((1, PAIRS, 2 * DK, c), lambda l: (l, 0, 0, 0))
    lane_tab = jax.ShapeDtypeStruct((DEPTH, TB_IN, D_QK), F32)
    lane_spec = pl.BlockSpec((1, TB_IN, D_QK), lambda l: (l, 0, 0))
    return pl.pallas_call(
        _tables_kernel,
        grid=(DEPTH,),
        in_specs=[pl.BlockSpec(memory_space=pltpu.SMEM)],
        out_specs=(pl.BlockSpec((1, PAIRS, c, 2 * c), lambda l: (l, 0, 0, 0)),
                   lane_spec, lane_spec, pair_spec, pair_spec, pair_spec, pair_spec),
        out_shape=(jax.ShapeDtypeStruct((DEPTH, PAIRS, c, 2 * c), F32),
                   lane_tab, lane_tab, pair_tab, pair_tab, pair_tab, pair_tab),
        compiler_params=_params(),
    )(dec)


def _inproj_kernel(seq_len, x_ref, g_ref, wqk_ref, w_ref, cos_ref, sin_ref, xif_ref, xib_ref,
                   zft_ref, dcf_ref, gn_ref, proj_ref, kt_ref, sf_ref, k_scr, stf):
    tb = x_ref.shape[0]

    @pl.when(lax.rem(pl.program_id(0) * tb, seq_len) == 0)
    def _():
        stf[...] = jnp.zeros_like(stf)

    h = _rms(x_ref[...], g_ref[...]).astype(BF16)
    cos = cos_ref[...]
    sin = sin_ref[...]

    qk = jnp.dot(h, wqk_ref[...], preferred_element_type=F32)
    for p in range(PAIRS):
        lo = p * 128
        qp = qk[:, lo:lo + 128]
        qr = (qp * cos + pltpu.roll(qp, 64, 1) * sin) * QK_SCALE
        proj_ref[:, OFF_Q + lo:OFF_Q + lo + 128] = qr.astype(BF16)
        proj_ref[:, OFF_QF + lo:OFF_QF + lo + 128] = (qr * xif_ref[:, lo:lo + 128]).astype(BF16)
        proj_ref[:, OFF_QB + lo:OFF_QB + lo + 128] = (qr * xib_ref[:, lo:lo + 128]).astype(BF16)
        kp = qk[:, D_QK + lo:D_QK + lo + 128]
        kr = kp * cos + pltpu.roll(kp, 64, 1) * sin
        k_scr[:, lo:lo + 128] = kr

    def section(w_off):
        return jnp.dot(h, w_ref[:, w_off:w_off + D_MODEL], preferred_element_type=F32)

    v = section(W_V).astype(BF16)
    proj_ref[:, OFF_V:OFF_V + D_MODEL] = v

    rl = lax.broadcasted_iota(jnp.int32, (2 * DK, DV), 0)
    even_row = ((rl >> 5) & 1) == 0

    def summaries(c):
        r0 = c * CHUNK
        for p in range(PAIRS):
            kt = k_scr[r0:r0 + CHUNK, p * 128:(p + 1) * 128].T
            kt_ref[c, p] = kt.astype(BF16)
            vp = v[r0:r0 + CHUNK, 2 * p * DV:(2 * p + 2) * DV]
            kv = jnp.dot((kt * zft_ref[p]).astype(BF16), vp, preferred_element_type=F32)
            cur = stf[p]
            sf_ref[c, p] = cur.astype(BF16)
            stf[p] = cur * dcf_ref[p] + jnp.where(even_row, kv[:, 0:DV], kv[:, DV:2 * DV])

    g = section(W_G)
    a_gate = g * jax.nn.sigmoid(g) * jax.nn.sigmoid(section(W_GR)) * gn_ref[...]
    proj_ref[:, OFF_A:OFF_A + D_MODEL] = a_gate.astype(BF16)
    proj_ref[:, OFF_U:OFF_U + D_MODEL] = section(W_U).astype(BF16)
    proj_ref[:, OFF_SGP:OFF_SGP + D_MODEL] = jax.nn.sigmoid(section(W_GP)).astype(BF16)
    for c in range(tb // CHUNK):
        summaries(c)


def _inproj(x, layer, w, tabs, cos_t, sin_t, seq_len):
    t = x.shape[0]
    tb = TB_IN
    nb = t // tb
    nb_seq = seq_len // tb
    cpb = tb // CHUNK
    xif, xib, zft, dcf = tabs
    row_l = pl.BlockSpec((None, 1, D_MODEL), lambda i: (layer, 0, 0))
    pair_l = pl.BlockSpec((None, PAIRS, 2 * DK, CHUNK), lambda i: (layer, 0, 0, 0))
    lane_l = pl.BlockSpec((None, tb, D_QK), lambda i: (layer, 0, 0))
    rope_blk = pl.BlockSpec((tb, 128), lambda i: (i % nb_seq, 0))
    state_blk = pl.BlockSpec((cpb, PAIRS, 2 * DK, DV), lambda i: (i, 0, 0, 0))
    return pl.pallas_call(
        functools.partial(_inproj_kernel, seq_len),
        grid=(nb,),
        in_specs=[
            pl.BlockSpec((tb, D_MODEL), lambda i: (i, 0)),
            row_l,
            pl.BlockSpec((None,) + w["w_qk"].shape[1:], lambda i: (layer, 0, 0), pipeline_mode=pl.Buffered(1)),
            pl.BlockSpec((None,) + w["w_rest"].shape[1:], lambda i: (layer, 0, 0), pipeline_mode=pl.Buffered(1)),
            rope_blk, rope_blk,
            lane_l, lane_l,
            pair_l, pair_l,
            row_l,
        ],
        out_specs=(pl.BlockSpec((tb, D_PROJ), lambda i: (i, 0)), state_blk, state_blk),
        out_shape=(
            jax.ShapeDtypeStruct((t, D_PROJ), BF16),
            jax.ShapeDtypeStruct((t // CHUNK, PAIRS, 2 * DK, CHUNK), BF16),
            jax.ShapeDtypeStruct((t // CHUNK, PAIRS, 2 * DK, DV), BF16),
        ),
        scratch_shapes=[pltpu.VMEM((tb, D_QK), F32), pltpu.VMEM((PAIRS, 2 * DK, DV), F32)],
        compiler_params=_params(),
    )(x, w["g_mix_pre"], w["w_qk"], w["w_rest"], cos_t, sin_t, xif, xib, zft, dcf, w["gn"])


def _mixer_kernel(seq_len, proj_ref, uprev_ref, unext_ref, kt_ref, sf_ref, zbt_ref, dcb_ref, m_ref, band_ref,
                  pw_ref, ps_ref, wout_ref, out_ref, yr_scr, yp_scr, u_scr, m_scr, stb):
    tb = proj_ref.shape[0]
    c = CHUNK
    step = pl.program_id(0)
    nblk = pl.num_programs(0) - 1
    blk = nblk - 1 - jnp.minimum(step, nblk - 1)
    s0 = lax.rem(blk * tb, seq_len)
    slot = lax.rem(step, 2)

    @pl.when(step == 0)
    def _():
        m_scr[1] = jnp.zeros(m_scr.shape[1:], BF16)

    @pl.when(s0 + tb == seq_len)
    def _():
        stb[...] = jnp.zeros_like(stb)

    def wout_piece(idx):
        rh, nt = divmod(idx, D_MODEL // WOUT_COLS)
        rows = slice(rh * WOUT_ROWS, (rh + 1) * WOUT_ROWS)
        cols = slice(nt * WOUT_COLS, (nt + 1) * WOUT_COLS)
        out_ref[rows, cols] = jnp.dot(m_scr[1 - slot, rows, :], wout_ref[:, cols], preferred_element_type=F32)

    keep_prev = jnp.where(s0 > 0, 1.0, 0.0)
    keep_next = jnp.where(s0 + tb < seq_len, 1.0, 0.0)
    u_scr[0:HALO, :] = (uprev_ref[...].astype(F32) * keep_prev).astype(BF16)
    u_scr[HALO:HALO + tb, :] = proj_ref[:, OFF_U:OFF_U + D_MODEL]
    u_scr[HALO + tb:2 * HALO + tb, :] = (unext_ref[...].astype(F32) * keep_next).astype(BF16)
    u_scr[2 * HALO + tb:, :] = jnp.zeros((u_scr.shape[0] - 2 * HALO - tb, D_MODEL), BF16)
    nsub = tb // POOL_SUB
    first_var = jnp.where(s0 == 0, BAND_FIRST, BAND_INTERIOR)
    last_var = jnp.where(s0 + tb == seq_len, BAND_LAST, BAND_INTERIOR)

    def pool_group(gi):
        lo = gi * POOL_GW
        parts = []
        for sb in range(nsub):
            r0 = sb * POOL_SUB
            var_idx = first_var if sb == 0 else (last_var if sb == nsub - 1 else BAND_INTERIOR)
            pg = jnp.dot(band_ref[var_idx, gi], u_scr[r0:r0 + POOL_WIN, lo:lo + POOL_GW],
                         preferred_element_type=F32)
            parts.append(pg.astype(BF16))
        yp = jnp.dot(jnp.concatenate(parts, axis=0), pw_ref[gi], preferred_element_type=F32)
        yp_scr[:, lo:lo + POOL_GW] = yp * ps_ref[:, lo:lo + POOL_GW]

    row = lax.broadcasted_iota(jnp.int32, (2 * DK, DV), 0)
    even_row = ((row >> 5) & 1) == 0
    row_even = even_row.astype(F32)
    s_mask = (row_even.astype(BF16), (1.0 - row_even).astype(BF16))
    zeros_v = jnp.zeros((c, DV), BF16)
    order = [(ci, p) for ci in reversed(range(tb // c)) for p in range(PAIRS)]

    def expand(w):
        return jnp.concatenate([w * s_mask[0], w * s_mask[1]], axis=1)

    def early(j):
        ci, p = order[j]
        rows = slice(ci * c, (ci + 1) * c)
        kt = kt_ref[ci, p]
        q = proj_ref[rows, OFF_Q + p * 128:OFF_Q + (p + 1) * 128]
        s = jnp.dot(q, expand(kt), preferred_element_type=F32)
        pm = (s * m_ref[p]).astype(BF16)
        vp = proj_ref[rows, OFF_V + 2 * p * DV:OFF_V + (2 * p + 2) * DV]
        kv = jnp.dot((kt.astype(F32) * zbt_ref[p]).astype(BF16), vp, preferred_element_type=F32)
        return pm, jnp.where(even_row, kv[:, 0:DV], kv[:, DV:2 * DV])

    def values(j, pm_kvb):
        pm, kvb = pm_kvb
        ci, p = order[j]
        rows = slice(ci * c, (ci + 1) * c)
        lo = p * 128
        qf = proj_ref[rows, OFF_QF + lo:OFF_QF + lo + 128]
        qb = proj_ref[rows, OFF_QB + lo:OFF_QB + lo + 128]
        v0 = proj_ref[rows, OFF_V + 2 * lo:OFF_V + 2 * lo + DV]
        v1 = proj_ref[rows, OFF_V + 2 * lo + DV:OFF_V + 2 * lo + 2 * DV]
        cur_b = stb[p]
        stb[p] = cur_b * dcb_ref[p] + kvb
        rhs = jnp.concatenate(
            [jnp.concatenate([v0, zeros_v], axis=1),
             jnp.concatenate([zeros_v, v1], axis=1),
             expand(sf_ref[ci, p]),
             expand(cur_b.astype(BF16))], axis=0)
        o2 = jnp.dot(jnp.concatenate([pm, qf, qb], axis=1), rhs, preferred_element_type=F32)
        for hh in range(2):
            hl = (2 * p + hh) * DV
            o = o2[:, hh * DV:(hh + 1) * DV]
            mu = jnp.mean(o, axis=-1, keepdims=True)
            d = o - mu
            var = jnp.mean(d * d, axis=-1, keepdims=True)
            a = proj_ref[rows, OFF_A + hl:OFF_A + hl + DV].astype(F32)
            yr_scr[rows, hl:hl + DV] = a * (d * lax.rsqrt(var + EPS))
        if p == PAIRS - 1:
            sgp = proj_ref[rows, OFF_SGP:OFF_SGP + D_MODEL].astype(F32)
            m_scr[slot, rows, :] = (yr_scr[rows, :] + sgp * yp_scr[rows, :]).astype(BF16)

    n_wout = (tb // WOUT_ROWS) * (D_MODEL // WOUT_COLS)
    fillers = {0: lambda: pool_group(0), 1: lambda: pool_group(1), 2: lambda: pool_group(2),
               3: lambda: pool_group(3)}
    for k in range(1, n_wout - 1):
        fillers[2 + 2 * k] = functools.partial(wout_piece, k)
    assert max(fillers) < len(order)

    wout_piece(0)
    pending = {0: early(0), 1: early(1)}
    for j in range(len(order)):
        if j in fillers:
            fillers[j]()
        values(j, pending.pop(j))
        if j + 2 < len(order):
            pending[j + 2] = early(j + 2)
    wout_piece(n_wout - 1)


def _pool_bands():
    i = np.arange(POOL_SUB)[:, None]
    j = np.arange(POOL_WIN)[None, :] - HALO
    out = np.zeros((3, len(POOL_WINDOWS), POOL_SUB, POOL_WIN), np.float64)
    for gi, w in enumerate(POOL_WINDOWS):
        half = w // 2
        inside = ((j - i >= -half) & (j - i <= half - 1)).astype(np.float64)
        token = (j == i).astype(np.float64)
        counts = {
            BAND_INTERIOR: np.full((POOL_SUB, 1), float(w)),
            BAND_FIRST: (i + half) - np.maximum(i - half, 0),
            BAND_LAST: np.minimum(i + half, POOL_SUB) - (i - half),
        }
        for var, cnt in counts.items():
            out[var, gi] = inside / cnt - token
    return out.astype(np.float32)


def _mixer(proj, kt, sf, layer, w, tabs, seq_len):
    t = proj.shape[0]
    tb = TB_MIX
    nb = t // tb
    cpb = tb // CHUNK
    hb = tb // HALO
    n_halo = t // HALO
    u_col = OFF_U // D_MODEL
    zbt, dcb, m_tab = tabs
    band = jnp.asarray(_pool_bands(), BF16)
    rev = lambda i: nb - 1 - jnp.minimum(i, nb - 1)
    rev_out = lambda i: nb - 1 - jnp.maximum(i - 1, 0)
    lay3 = lambda i: (layer, 0, 0)
    lay4 = lambda i: (layer, 0, 0, 0)
    state_blk = pl.BlockSpec((cpb, PAIRS, 2 * DK, DV), lambda i: (rev(i), 0, 0, 0))
    return pl.pallas_call(
        functools.partial(_mixer_kernel, seq_len),
        grid=(nb + 1,),
        in_specs=[
            pl.BlockSpec((tb, D_PROJ), lambda i: (rev(i), 0)),
            pl.BlockSpec((HALO, D_MODEL), lambda i: (jnp.maximum(rev(i) * hb - 1, 0), u_col)),
            pl.BlockSpec((HALO, D_MODEL), lambda i: (jnp.minimum((rev(i) + 1) * hb, n_halo - 1), u_col)),
            state_blk, state_blk,
            pl.BlockSpec((None, PAIRS, 2 * DK, CHUNK), lay4),
            pl.BlockSpec((None, PAIRS, 2 * DK, DV), lay4),
            pl.BlockSpec((None, PAIRS, CHUNK, 2 * CHUNK), lay4),
            pl.BlockSpec(band.shape, lambda i: (0, 0, 0, 0)),
            pl.BlockSpec((None, len(POOL_WINDOWS), POOL_GW, POOL_GW), lay4),
            pl.BlockSpec((None, 1, D_MODEL), lay3),
            pl.BlockSpec((None, D_MODEL, D_MODEL), lay3),
        ],
        out_specs=pl.BlockSpec((tb, D_MODEL), lambda i: (rev_out(i), 0)),
        out_shape=jax.ShapeDtypeStruct((t, D_MODEL), F32),
        scratch_shapes=[
            pltpu.VMEM((tb, D_MODEL), F32),
            pltpu.VMEM((tb, D_MODEL), F32),
            pltpu.VMEM((tb - POOL_SUB + POOL_WIN, D_MODEL), BF16),
            pltpu.VMEM((2, tb, D_MODEL), BF16),
            pltpu.VMEM((PAIRS, 2 * DK, DV), F32),
        ],
        compiler_params=_params(),
    )(proj, proj, proj, kt, sf, zbt, dcb, m_tab, band, w["pool_w"], w["pool_scale"], w["w_out"])


def _mlp_body(x_ref, z_ref, gmix_ref, gpre_ref, w1_ref, w2_ref, gpost_ref, out_ref, x1_rd, h_rd, x1_wr, h_wr):
    tb = x_ref.shape[0]
    n_ff = D_FF // D_MODEL
    rows_per = tb // (2 * n_ff)

    def norms(k):
        rows = slice(k * rows_per, (k + 1) * rows_per)
        x1 = x_ref[rows, :] + _rms(z_ref[rows, :], gmix_ref[...])
        x1_wr[rows, :] = x1
        h_wr[rows, :] = _rms(x1, gpre_ref[...]).astype(BF16)
        return _zero_like_row(x1[0:1, :])

    h = h_rd[...]
    acc = jnp.zeros((tb, D_MODEL), F32)
    for j in range(n_ff):
        lo = j * D_MODEL
        floor = jnp.maximum(norms(2 * j), norms(2 * j + 1))
        f = jnp.dot(h, w1_ref[:, lo:lo + D_MODEL], preferred_element_type=F32)
        f = jnp.square(jnp.maximum(f, floor)).astype(BF16)
        acc = acc + jnp.dot(f, w2_ref[lo:lo + D_MODEL, :], preferred_element_type=F32)
    out_ref[...] = x1_rd[...] + _rms(acc, gpost_ref[...])


def _mlp_kernel(x_ref, z_ref, gmix_ref, gpre_ref, w1_ref, w2_ref, gpost_ref, out_ref, x1_a, h_a, x1_b, h_b):
    step = pl.program_id(0)
    args = (x_ref, z_ref, gmix_ref, gpre_ref, w1_ref, w2_ref, gpost_ref, out_ref)

    @pl.when(step == 0)
    def _():
        x1_b[...] = jnp.zeros_like(x1_b)
        h_b[...] = jnp.zeros_like(h_b)

    @pl.when(lax.rem(step, 2) == 0)
    def _():
        _mlp_body(*args, x1_b, h_b, x1_a, h_a)

    @pl.when(lax.rem(step, 2) == 1)
    def _():
        _mlp_body(*args, x1_a, h_a, x1_b, h_b)


def _mlp(x, z, layer, w):
    t = x.shape[0]
    tb = TB_MLP
    nb = t // tb
    lay3 = lambda i: (layer, 0, 0)
    row_l = pl.BlockSpec((None, 1, D_MODEL), lay3)
    cur = lambda i: (jnp.minimum(i, nb - 1), 0)
    return pl.pallas_call(
        _mlp_kernel,
        grid=(nb + 1,),
        in_specs=[
            pl.BlockSpec((tb, D_MODEL), cur),
            pl.BlockSpec((tb, D_MODEL), cur),
            row_l, row_l,
            pl.BlockSpec((None, D_MODEL, D_FF), lay3, pipeline_mode=pl.Buffered(1)),
            pl.BlockSpec((None, D_FF, D_MODEL), lay3, pipeline_mode=pl.Buffered(1)),
            row_l,
        ],
        out_specs=pl.BlockSpec((tb, D_MODEL), lambda i: (jnp.maximum(i - 1, 0), 0)),
        out_shape=jax.ShapeDtypeStruct((t, D_MODEL), F32),
        scratch_shapes=[pltpu.VMEM((tb, D_MODEL), F32), pltpu.VMEM((tb, D_MODEL), BF16),
                        pltpu.VMEM((tb, D_MODEL), F32), pltpu.VMEM((tb, D_MODEL), BF16)],
        compiler_params=_params(),
    )(x, z, w["g_mix_post"], w["g_mlp_pre"], w["w_mlp1"], w["w_mlp2"], w["g_mlp_post"])


def _qk_column_order():
    n = np.arange(D_QK)
    pair, l = n // 128, n % 128
    head = 2 * pair + (l // 32) % 2
    return head * DK + (l // 64) * (DK // 2) + l % 32


def _rope_tables(seq_len):
    half = DK // 2
    inv = ROPE_BASE ** (-jnp.arange(half, dtype=F32) / half)
    ang = jnp.arange(seq_len, dtype=F32)[:, None] * inv[None, :]
    cos, sin = jnp.cos(ang), jnp.sin(ang)
    return jnp.tile(cos, (1, 4)), jnp.concatenate([-sin, -sin, sin, sin], axis=1)


def _trunk(x, w, tabs, cos_t, sin_t):
    batch, seq_len, _ = x.shape
    xt = x.reshape(batch * seq_len, D_MODEL)
    m_tab, xif, xib, zft, zbt, dcf, dcb = tabs
    for layer in range(DEPTH):
        proj, kt, sf = _inproj(xt, layer, w, (xif, xib, zft, dcf), cos_t, sin_t, seq_len)
        z = _mixer(proj, kt, sf, layer, w, (zbt, dcb, m_tab), seq_len)
        xt = _mlp(xt, z, layer, w)
    return xt.reshape(batch, seq_len, D_MODEL)


def kernel(x_prompt, x_sample, norm_mix_pre, norm_mix_post, w_in, ret_decay_fwd, ret_decay_bwd, ret_gn,
           pool_w, pool_scale, w_out, norm_mlp_pre, norm_mlp_post, w_mlp1, w_mlp2):
    order = _qk_column_order()
    qk_cols = np.concatenate([order, D_QK + order])
    w = dict(
        w_qk=w_in[:, :, :2 * D_QK][:, :, qk_cols].astype(BF16), w_rest=w_in[:, :, 2 * D_QK:].astype(BF16),
        pool_w=pool_w.astype(BF16), w_out=w_out.astype(BF16),
        w_mlp1=w_mlp1.astype(BF16), w_mlp2=w_mlp2.astype(BF16),
        g_mix_pre=norm_mix_pre[:, None, :], g_mix_post=norm_mix_post[:, None, :], gn=ret_gn[:, None, :],
        pool_scale=pool_scale[:, None, :], g_mlp_pre=norm_mlp_pre[:, None, :], g_mlp_post=norm_mlp_post[:, None, :])
    dec = jnp.concatenate([ret_decay_fwd, ret_decay_bwd], axis=1).astype(F32)
    tabs = _decay_tables(dec)
    cos_t, sin_t = _rope_tables(max(x_prompt.shape[1], x_sample.shape[1]))
    y_prompt = _trunk(x_prompt, w, tabs, cos_t, sin_t)
    y_sample = _trunk(x_sample, w, tabs, cos_t, sin_t)
    return (y_prompt, y_sample)
```

```python
import functools

import numpy as np
import jax
import jax.numpy as jnp
from jax import lax
from jax.experimental import pallas as pl
from jax.experimental.pallas import tpu as pltpu

D_MODEL = 1024
DEPTH = 4
HEADS = 8
DK = 64
DV = 128
PAIRS = HEADS // 2
D_QK = HEADS * DK
CHUNK = 128
ROPE_BASE = 10000.0
POOL_WINDOWS = (2, 4, 8, 16)
POOL_GW = 256
D_FF = 4 * D_MODEL
EPS = 1e-6
QK_SCALE = DK ** -0.5

OFF_U, OFF_V, OFF_A, OFF_SGP, OFF_Q, OFF_QF, OFF_QB = 0, 1024, 2048, 3072, 4096, 4608, 5120
D_PROJ = 5632
W_V, W_G, W_U, W_GR, W_GP = 1024, 2048, 3072, 4096, 5120
HALO = 16
POOL_SUB = 128
POOL_WIN = 256
BAND_INTERIOR, BAND_FIRST, BAND_LAST = 0, 1, 2

TB_IN = 512
TB_MIX = 512
WOUT_ROWS, WOUT_COLS = 256, 256
TB_MLP = 512
MLP_SPLIT = 1
VMEM_LIMIT_BYTES = 56 * 1024 * 1024

F32 = jnp.float32
BF16 = jnp.bfloat16


def _rms(x, g):
    return x * lax.rsqrt(jnp.mean(x * x, axis=-1, keepdims=True) + EPS) * g


def _zero_like_row(row):
    bits = lax.bitcast_convert_type(row, jnp.uint32)
    return lax.bitcast_convert_type((bits >> 16) >> 16, F32)


def _log_sigmoid(d):
    return jnp.minimum(d, 0.0) - jnp.log1p(jnp.exp(-jnp.abs(d)))


def _params():
    return pltpu.CompilerParams(dimension_semantics=("arbitrary",), vmem_limit_bytes=VMEM_LIMIT_BYTES)


def _tables_kernel(dec_ref, m_ref, xif_ref, xib_ref, zft_ref, zbt_ref, dcf_ref, dcb_ref):
    l = pl.program_id(0)
    c = CHUNK
    df = [dec_ref[l, h] for h in range(HEADS)]
    db = [dec_ref[l, HEADS + h] for h in range(HEADS)]

    ii = lax.broadcasted_iota(jnp.int32, (c, c), 0)
    jj = lax.broadcasted_iota(jnp.int32, (c, c), 1)
    dist = jnp.abs(ii - jj).astype(F32)
    for h in range(HEADS):
        dsel = jnp.where(ii >= jj, jnp.full((c, c), df[h], F32), jnp.full((c, c), db[h], F32))
        m_ref[0, h // 2, :, (h % 2) * c:(h % 2 + 1) * c] = jnp.exp(dist * _log_sigmoid(dsel))

    rows = xif_ref.shape[1]
    lane = lax.broadcasted_iota(jnp.int32, (rows, D_QK), 1)
    head = 2 * (lane >> 7) + ((lane >> 5) & 1)
    idx = (lax.broadcasted_iota(jnp.int32, (rows, D_QK), 0) & (c - 1)).astype(F32)
    sel_f = jnp.full((rows, D_QK), df[0], F32)
    sel_b = jnp.full((rows, D_QK), db[0], F32)
    for h in range(1, HEADS):
        sel_f = jnp.where(head == h, df[h], sel_f)
        sel_b = jnp.where(head == h, db[h], sel_b)
    xif_ref[0] = jnp.exp((idx + 1.0) * _log_sigmoid(sel_f))
    xib_ref[0] = jnp.exp((c - idx) * _log_sigmoid(sel_b))

    rl = lax.broadcasted_iota(jnp.int32, (2 * DK, c), 0)
    odd = ((rl >> 5) & 1) == 1
    jc = lax.broadcasted_iota(jnp.int32, (2 * DK, c), 1).astype(F32)
    for p in range(PAIRS):
        lg_f = _log_sigmoid(jnp.where(odd, df[2 * p + 1], df[2 * p]))
        lg_b = _log_sigmoid(jnp.where(odd, db[2 * p + 1], db[2 * p]))
        zft_ref[0, p] = jnp.exp((c - 1.0 - jc) * lg_f)
        zbt_ref[0, p] = jnp.exp(jc * lg_b)
        dcf_ref[0, p] = jnp.exp(float(c) * lg_f)
        dcb_ref[0, p] = jnp.exp(float(c) * lg_b)


def _decay_tables(dec):
    c = CHUNK
    pair_tab = jax.ShapeDtypeStruct((DEPTH, PAIRS, 2 * DK, c), F32)
    pair_spec = pl.BlockSpec((1, PAIRS, 2 * DK, c), lambda l: (l, 0, 0, 0))
    lane_tab = jax.ShapeDtypeStruct((DEPTH, TB_IN, D_QK), F32)
    lane_spec = pl.BlockSpec((1, TB_IN, D_QK), lambda l: (l, 0, 0))
    return pl.pallas_call(
        _tables_kernel,
        grid=(DEPTH,),
        in_specs=[pl.BlockSpec(memory_space=pltpu.SMEM)],
        out_specs=(pl.BlockSpec((1, PAIRS, c, 2 * c), lambda l: (l, 0, 0, 0)),
                   lane_spec, lane_spec, pair_spec, pair_spec, pair_spec, pair_spec),
        out_shape=(jax.ShapeDtypeStruct((DEPTH, PAIRS, c, 2 * c), F32),
                   lane_tab, lane_tab, pair_tab, pair_tab, pair_tab, pair_tab),
        compiler_params=_params(),
    )(dec)


def _fold_pool_kernel(wu_ref, pw_ref, ps_ref, out_ref):
    prod = jnp.dot(wu_ref[...], pw_ref[...], preferred_element_type=F32, precision=lax.Precision.HIGHEST)
    out_ref[...] = (prod * ps_ref[...]).astype(out_ref.dtype)


def _fold_pool(w_in, pool_w, pool_scale):
    n_groups = len(POOL_WINDOWS)
    u_blk = W_U // POOL_GW
    return pl.pallas_call(
        _fold_pool_kernel,
        grid=(DEPTH, n_groups),
        in_specs=[
            pl.BlockSpec((None, D_MODEL, POOL_GW), lambda l, g: (l, 0, u_blk + g)),
            pl.BlockSpec((None, None, POOL_GW, POOL_GW), lambda l, g: (l, g, 0, 0)),
            pl.BlockSpec((None, 1, POOL_GW), lambda l, g: (l, 0, g)),
        ],
        out_specs=pl.BlockSpec((None, D_MODEL, POOL_GW), lambda l, g: (l, 0, g)),
        out_shape=jax.ShapeDtypeStruct((DEPTH, D_MODEL, D_MODEL), BF16),
        compiler_params=pltpu.CompilerParams(dimension_semantics=("arbitrary", "arbitrary"),
                                             vmem_limit_bytes=VMEM_LIMIT_BYTES),
    )(w_in, pool_w, pool_scale[:, None, :])


def _inproj_kernel(seq_len, x_ref, g_ref, wqk_ref, w_ref, wu_ref, cos_ref, sin_ref, xif_ref, xib_ref,
                   zft_ref, dcf_ref, gn_ref, proj_ref, kt_ref, sf_ref, k_scr, stf):
    tb = x_ref.shape[0]

    @pl.when(lax.rem(pl.program_id(0) * tb, seq_len) == 0)
    def _():
        stf[...] = jnp.zeros_like(stf)

    h = _rms(x_ref[...], g_ref[...]).astype(BF16)
    cos = cos_ref[...]
    sin = sin_ref[...]

    qk = jnp.dot(h, wqk_ref[...], preferred_element_type=F32)
    for p in range(PAIRS):
        lo = p * 128
        qp = qk[:, lo:lo + 128]
        qr = (qp * cos + pltpu.roll(qp, 64, 1) * sin) * QK_SCALE
        proj_ref[:, OFF_Q + lo:OFF_Q + lo + 128] = qr.astype(BF16)
        proj_ref[:, OFF_QF + lo:OFF_QF + lo + 128] = (qr * xif_ref[:, lo:lo + 128]).astype(BF16)
        proj_ref[:, OFF_QB + lo:OFF_QB + lo + 128] = (qr * xib_ref[:, lo:lo + 128]).astype(BF16)
        kp = qk[:, D_QK + lo:D_QK + lo + 128]
        kr = kp * cos + pltpu.roll(kp, 64, 1) * sin
        k_scr[:, lo:lo + 128] = kr

    def section(w_off):
        return jnp.dot(h, w_ref[:, w_off:w_off + D_MODEL], preferred_element_type=F32)

    v = section(W_V).astype(BF16)
    proj_ref[:, OFF_V:OFF_V + D_MODEL] = v

    rl = lax.broadcasted_iota(jnp.int32, (2 * DK, DV), 0)
    even_row = ((rl >> 5) & 1) == 0

    def summaries(c):
        r0 = c * CHUNK
        for p in range(PAIRS):
            kt = k_scr[r0:r0 + CHUNK, p * 128:(p + 1) * 128].T
            kt_ref[c, p] = kt.astype(BF16)
            vp = v[r0:r0 + CHUNK, 2 * p * DV:(2 * p + 2) * DV]
            kv = jnp.dot((kt * zft_ref[p]).astype(BF16), vp, preferred_element_type=F32)
            cur = stf[p]
            sf_ref[c, p] = cur.astype(BF16)
            stf[p] = cur * dcf_ref[p] + jnp.where(even_row, kv[:, 0:DV], kv[:, DV:2 * DV])

    g = section(W_G)
    a_gate = g * jax.nn.sigmoid(g) * jax.nn.sigmoid(section(W_GR)) * gn_ref[...]
    proj_ref[:, OFF_A:OFF_A + D_MODEL] = a_gate.astype(BF16)
    proj_ref[:, OFF_U:OFF_U + D_MODEL] = jnp.dot(h, wu_ref[...], preferred_element_type=F32).astype(BF16)
    proj_ref[:, OFF_SGP:OFF_SGP + D_MODEL] = jax.nn.sigmoid(section(W_GP)).astype(BF16)
    for c in range(tb // CHUNK):
        summaries(c)


def _inproj(x, layer, w, tabs, cos_t, sin_t, seq_len):
    t = x.shape[0]
    tb = TB_IN
    nb = t // tb
    nb_seq = seq_len // tb
    cpb = tb // CHUNK
    xif, xib, zft, dcf = tabs
    row_l = pl.BlockSpec((None, 1, D_MODEL), lambda i: (layer, 0, 0))
    pair_l = pl.BlockSpec((None, PAIRS, 2 * DK, CHUNK), lambda i: (layer, 0, 0, 0))
    lane_l = pl.BlockSpec((None, tb, D_QK), lambda i: (layer, 0, 0))
    rope_blk = pl.BlockSpec((tb, 128), lambda i: (i % nb_seq, 0))
    state_blk = pl.BlockSpec((cpb, PAIRS, 2 * DK, DV), lambda i: (i, 0, 0, 0))
    return pl.pallas_call(
        functools.partial(_inproj_kernel, seq_len),
        grid=(nb,),
        in_specs=[
            pl.BlockSpec((tb, D_MODEL), lambda i: (i, 0)),
            row_l,
            pl.BlockSpec((None,) + w["w_qk"].shape[1:], lambda i: (layer, 0, 0), pipeline_mode=pl.Buffered(1)),
            pl.BlockSpec((None,) + w["w_in"].shape[1:], lambda i: (layer, 0, 0), pipeline_mode=pl.Buffered(1)),
            pl.BlockSpec((None, D_MODEL, D_MODEL), lambda i: (layer, 0, 0), pipeline_mode=pl.Buffered(1)),
            rope_blk, rope_blk,
            lane_l, lane_l,
            pair_l, pair_l,
            row_l,
        ],
        out_specs=(pl.BlockSpec((tb, D_PROJ), lambda i: (i, 0)), state_blk, state_blk),
        out_shape=(
            jax.ShapeDtypeStruct((t, D_PROJ), BF16),
            jax.ShapeDtypeStruct((t // CHUNK, PAIRS, 2 * DK, CHUNK), BF16),
            jax.ShapeDtypeStruct((t // CHUNK, PAIRS, 2 * DK, DV), BF16),
        ),
        scratch_shapes=[pltpu.VMEM((tb, D_QK), F32), pltpu.VMEM((PAIRS, 2 * DK, DV), F32)],
        compiler_params=_params(),
    )(x, w["g_mix_pre"], w["w_qk"], w["w_in"], w["w_u"], cos_t, sin_t, xif, xib, zft, dcf, w["gn"])


def _mixer_kernel(seq_len, proj_ref, uprev_ref, unext_ref, kt_ref, sf_ref, zbt_ref, dcb_ref, m_ref, band_ref,
                  wout_ref, out_ref, yr_scr, yp_scr, u_scr, m_scr, stb):
    tb = proj_ref.shape[0]
    c = CHUNK
    step = pl.program_id(0)
    nblk = pl.num_programs(0) - 1
    blk = nblk - 1 - jnp.minimum(step, nblk - 1)
    s0 = lax.rem(blk * tb, seq_len)
    slot = lax.rem(step, 2)

    @pl.when(step == 0)
    def _():
        m_scr[1] = jnp.zeros(m_scr.shape[1:], BF16)

    @pl.when(s0 + tb == seq_len)
    def _():
        stb[...] = jnp.zeros_like(stb)

    def wout_piece(idx):
        rh, nt = divmod(idx, D_MODEL // WOUT_COLS)
        rows = slice(rh * WOUT_ROWS, (rh + 1) * WOUT_ROWS)
        cols = slice(nt * WOUT_COLS, (nt + 1) * WOUT_COLS)
        out_ref[rows, cols] = jnp.dot(m_scr[1 - slot, rows, :], wout_ref[:, cols], preferred_element_type=F32)

    keep_prev = jnp.where(s0 > 0, 1.0, 0.0)
    keep_next = jnp.where(s0 + tb < seq_len, 1.0, 0.0)
    u_scr[0:HALO, :] = (uprev_ref[...].astype(F32) * keep_prev).astype(BF16)
    u_scr[HALO:HALO + tb, :] = proj_ref[:, OFF_U:OFF_U + D_MODEL]
    u_scr[HALO + tb:2 * HALO + tb, :] = (unext_ref[...].astype(F32) * keep_next).astype(BF16)
    u_scr[2 * HALO + tb:, :] = jnp.zeros((u_scr.shape[0] - 2 * HALO - tb, D_MODEL), BF16)
    nsub = tb // POOL_SUB
    first_var = jnp.where(s0 == 0, BAND_FIRST, BAND_INTERIOR)
    last_var = jnp.where(s0 + tb == seq_len, BAND_LAST, BAND_INTERIOR)

    def pool_group(gi):
        lo = gi * POOL_GW
        for sb in range(nsub):
            r0 = sb * POOL_SUB
            var_idx = first_var if sb == 0 else (last_var if sb == nsub - 1 else BAND_INTERIOR)
            yp_scr[r0:r0 + POOL_SUB, lo:lo + POOL_GW] = jnp.dot(
                band_ref[var_idx, gi], u_scr[r0:r0 + POOL_WIN, lo:lo + POOL_GW], preferred_element_type=F32)

    row = lax.broadcasted_iota(jnp.int32, (2 * DK, DV), 0)
    even_row = ((row >> 5) & 1) == 0
    row_even = even_row.astype(F32)
    s_mask = (row_even.astype(BF16), (1.0 - row_even).astype(BF16))
    zeros_v = jnp.zeros((c, DV), BF16)
    order = [(ci, p) for ci in reversed(range(tb // c)) for p in range(PAIRS)]

    def expand(w):
        return jnp.concatenate([w * s_mask[0], w * s_mask[1]], axis=1)

    def early(j):
        ci, p = order[j]
        rows = slice(ci * c, (ci + 1) * c)
        kt = kt_ref[ci, p]
        q = proj_ref[rows, OFF_Q + p * 128:OFF_Q + (p + 1) * 128]
        s = jnp.dot(q, expand(kt), preferred_element_type=F32)
        pm = (s * m_ref[p]).astype(BF16)
        vp = proj_ref[rows, OFF_V + 2 * p * DV:OFF_V + (2 * p + 2) * DV]
        kv = jnp.dot((kt.astype(F32) * zbt_ref[p]).astype(BF16), vp, preferred_element_type=F32)
        return pm, jnp.where(even_row, kv[:, 0:DV], kv[:, DV:2 * DV])

    def values(j, pm_kvb):
        pm, kvb = pm_kvb
        ci, p = order[j]
        rows = slice(ci * c, (ci + 1) * c)
        lo = p * 128
        qf = proj_ref[rows, OFF_QF + lo:OFF_QF + lo + 128]
        qb = proj_ref[rows, OFF_QB + lo:OFF_QB + lo + 128]
        v0 = proj_ref[rows, OFF_V + 2 * lo:OFF_V + 2 * lo + DV]
        v1 = proj_ref[rows, OFF_V + 2 * lo + DV:OFF_V + 2 * lo + 2 * DV]
        cur_b = stb[p]
        stb[p] = cur_b * dcb_ref[p] + kvb
        rhs = jnp.concatenate(
            [jnp.concatenate([v0, zeros_v], axis=1),
             jnp.concatenate([zeros_v, v1], axis=1),
             expand(sf_ref[ci, p]),
             expand(cur_b.astype(BF16))], axis=0)
        o2 = jnp.dot(jnp.concatenate([pm, qf, qb], axis=1), rhs, preferred_element_type=F32)
        for hh in range(2):
            hl = (2 * p + hh) * DV
            o = o2[:, hh * DV:(hh + 1) * DV]
            mu = jnp.mean(o, axis=-1, keepdims=True)
            d = o - mu
            var = jnp.mean(d * d, axis=-1, keepdims=True)
            a = proj_ref[rows, OFF_A + hl:OFF_A + hl + DV].astype(F32)
            yr_scr[rows, hl:hl + DV] = a * (d * lax.rsqrt(var + EPS))
        if p == PAIRS - 1:
            sgp = proj_ref[rows, OFF_SGP:OFF_SGP + D_MODEL].astype(F32)
            m_scr[slot, rows, :] = (yr_scr[rows, :] + sgp * yp_scr[rows, :]).astype(BF16)

    n_wout = (tb // WOUT_ROWS) * (D_MODEL // WOUT_COLS)
    fillers = {0: lambda: pool_group(0), 1: lambda: pool_group(1), 2: lambda: pool_group(2),
               3: lambda: pool_group(3)}
    for k in range(1, n_wout - 1):
        fillers[2 + 2 * k] = functools.partial(wout_piece, k)
    assert max(fillers) < len(order)

    wout_piece(0)
    pending = {0: early(0), 1: early(1)}
    for j in range(len(order)):
        if j in fillers:
            fillers[j]()
        values(j, pending.pop(j))
        if j + 2 < len(order):
            pending[j + 2] = early(j + 2)
    wout_piece(n_wout - 1)


def _pool_bands():
    i = np.arange(POOL_SUB)[:, None]
    j = np.arange(POOL_WIN)[None, :] - HALO
    out = np.zeros((3, len(POOL_WINDOWS), POOL_SUB, POOL_WIN), np.float64)
    for gi, w in enumerate(POOL_WINDOWS):
        half = w // 2
        inside = ((j - i >= -half) & (j - i <= half - 1)).astype(np.float64)
        token = (j == i).astype(np.float64)
        counts = {
            BAND_INTERIOR: np.full((POOL_SUB, 1), float(w)),
            BAND_FIRST: (i + half) - np.maximum(i - half, 0),
            BAND_LAST: np.minimum(i + half, POOL_SUB) - (i - half),
        }
        for var, cnt in counts.items():
            out[var, gi] = inside / cnt - token
    return out.astype(np.float32)


def _mixer(proj, kt, sf, layer, w, tabs, seq_len):
    t = proj.shape[0]
    tb = TB_MIX
    nb = t // tb
    cpb = tb // CHUNK
    hb = tb // HALO
    n_halo = t // HALO
    u_col = OFF_U // D_MODEL
    zbt, dcb, m_tab = tabs
    band = jnp.asarray(_pool_bands(), BF16)
    rev = lambda i: nb - 1 - jnp.minimum(i, nb - 1)
    rev_out = lambda i: nb - 1 - jnp.maximum(i - 1, 0)
    lay3 = lambda i: (layer, 0, 0)
    lay4 = lambda i: (layer, 0, 0, 0)
    state_blk = pl.BlockSpec((cpb, PAIRS, 2 * DK, DV), lambda i: (rev(i), 0, 0, 0))
    return pl.pallas_call(
        functools.partial(_mixer_kernel, seq_len),
        grid=(nb + 1,),
        in_specs=[
            pl.BlockSpec((tb, D_PROJ), lambda i: (rev(i), 0)),
            pl.BlockSpec((HALO, D_MODEL), lambda i: (jnp.maximum(rev(i) * hb - 1, 0), u_col)),
            pl.BlockSpec((HALO, D_MODEL), lambda i: (jnp.minimum((rev(i) + 1) * hb, n_halo - 1), u_col)),
            state_blk, state_blk,
            pl.BlockSpec((None, PAIRS, 2 * DK, CHUNK), lay4),
            pl.BlockSpec((None, PAIRS, 2 * DK, DV), lay4),
            pl.BlockSpec((None, PAIRS, CHUNK, 2 * CHUNK), lay4),
            pl.BlockSpec(band.shape, lambda i: (0, 0, 0, 0)),
            pl.BlockSpec((None, D_MODEL, D_MODEL), lay3),
        ],
        out_specs=pl.BlockSpec((tb, D_MODEL), lambda i: (rev_out(i), 0)),
        out_shape=jax.ShapeDtypeStruct((t, D_MODEL), F32),
        scratch_shapes=[
            pltpu.VMEM((tb, D_MODEL), F32),
            pltpu.VMEM((tb, D_MODEL), F32),
            pltpu.VMEM((tb - POOL_SUB + POOL_WIN, D_MODEL), BF16),
            pltpu.VMEM((2, tb, D_MODEL), BF16),
            pltpu.VMEM((PAIRS, 2 * DK, DV), F32),
        ],
        compiler_params=_params(),
    )(proj, proj, proj, kt, sf, zbt, dcb, m_tab, band, w["w_out"])


def _mlp_body(x_ref, z_ref, gmix_ref, gpre_ref, w1_ref, w2_ref, gpost_ref, out_ref, x1_rd, h_rd, x1_wr, h_wr):
    tb = x_ref.shape[0]
    n_ff = D_FF // D_MODEL
    rows_per = tb // (2 * n_ff)

    def norms(k):
        rows = slice(k * rows_per, (k + 1) * rows_per)
        x1 = x_ref[rows, :] + _rms(z_ref[rows, :], gmix_ref[...])
        x1_wr[rows, :] = x1
        h_wr[rows, :] = _rms(x1, gpre_ref[...]).astype(BF16)
        return _zero_like_row(x1[0:1, :])

    h = h_rd[...]
    acc = jnp.zeros((tb, D_MODEL), F32)
    for j in range(n_ff):
        lo = j * D_MODEL
        floor = jnp.maximum(norms(2 * j), norms(2 * j + 1))
        f = jnp.dot(h, w1_ref[:, lo:lo + D_MODEL], preferred_element_type=F32)
        f = jnp.square(jnp.maximum(f, floor)).astype(BF16)
        acc = acc + jnp.dot(f, w2_ref[lo:lo + D_MODEL, :], preferred_element_type=F32)
    out_ref[...] = x1_rd[...] + _rms(acc, gpost_ref[...])


def _mlp_kernel(x_ref, z_ref, gmix_ref, gpre_ref, w1_ref, w2_ref, gpost_ref, out_ref, x1_a, h_a, x1_b, h_b):
    step = pl.program_id(0)
    args = (x_ref, z_ref, gmix_ref, gpre_ref, w1_ref, w2_ref, gpost_ref, out_ref)

    @pl.when(step == 0)
    def _():
        x1_b[...] = jnp.zeros_like(x1_b)
        h_b[...] = jnp.zeros_like(h_b)

    @pl.when(lax.rem(step, 2) == 0)
    def _():
        _mlp_body(*args, x1_b, h_b, x1_a, h_a)

    @pl.when(lax.rem(step, 2) == 1)
    def _():
        _mlp_body(*args, x1_a, h_a, x1_b, h_b)


def _mlp(x, z, layer, w):
    t = x.shape[0]
    tb = TB_MLP
    nb = t // tb
    lay3 = lambda i: (layer, 0, 0)
    row_l = pl.BlockSpec((None, 1, D_MODEL), lay3)
    cur = lambda i: (jnp.minimum(i, nb - 1), 0)
    return pl.pallas_call(
        _mlp_kernel,
        grid=(nb + 1,),
        in_specs=[
            pl.BlockSpec((tb, D_MODEL), cur),
            pl.BlockSpec((tb, D_MODEL), cur),
            row_l, row_l,
            pl.BlockSpec((None, D_MODEL, D_FF), lay3, pipeline_mode=pl.Buffered(1)),
            pl.BlockSpec((None, D_FF, D_MODEL), lay3, pipeline_mode=pl.Buffered(1)),
            row_l,
        ],
        out_specs=pl.BlockSpec((tb, D_MODEL), lambda i: (jnp.maximum(i - 1, 0), 0)),
        out_shape=jax.ShapeDtypeStruct((t, D_MODEL), F32),
        scratch_shapes=[pltpu.VMEM((tb, D_MODEL), F32), pltpu.VMEM((tb, D_MODEL), BF16),
                        pltpu.VMEM((tb, D_MODEL), F32), pltpu.VMEM((tb, D_MODEL), BF16)],
        compiler_params=_params(),
    )(x, z, w["g_mix_post"], w["g_mlp_pre"], w["w_mlp1"], w["w_mlp2"], w["g_mlp_post"])


def _qk_column_order():
    n = np.arange(D_QK)
    pair, l = n // 128, n % 128
    head = 2 * pair + (l // 32) % 2
    return head * DK + (l // 64) * (DK // 2) + l % 32


def _rope_tables(seq_len):
    half = DK // 2
    inv = ROPE_BASE ** (-jnp.arange(half, dtype=F32) / half)
    ang = jnp.arange(seq_len, dtype=F32)[:, None] * inv[None, :]
    cos, sin = jnp.cos(ang), jnp.sin(ang)
    return jnp.tile(cos, (1, 4)), jnp.concatenate([-sin, -sin, sin, sin], axis=1)


def _trunk(x, w, tabs, cos_t, sin_t):
    batch, seq_len, _ = x.shape
    xt = x.reshape(batch * seq_len, D_MODEL)
    m_tab, xif, xib, zft, zbt, dcf, dcb = tabs
    for layer in range(DEPTH):
        proj, kt, sf = _inproj(xt, layer, w, (xif, xib, zft, dcf), cos_t, sin_t, seq_len)
        z = _mixer(proj, kt, sf, layer, w, (zbt, dcb, m_tab), seq_len)
        xt = _mlp(xt, z, layer, w)
    return xt.reshape(batch, seq_len, D_MODEL)


def kernel(x_prompt, x_sample, norm_mix_pre, norm_mix_post, w_in, ret_decay_fwd, ret_decay_bwd, ret_gn,
           pool_w, pool_scale, w_out, norm_mlp_pre, norm_mlp_post, w_mlp1, w_mlp2):
    order = _qk_column_order()
    qk_cols = np.concatenate([order, D_QK + order])
    w = dict(
        w_qk=w_in[:, :, :2 * D_QK][:, :, qk_cols].astype(BF16), w_in=w_in.astype(BF16),
        w_u=_fold_pool(w_in, pool_w, pool_scale), w_out=w_out.astype(BF16),
        w_mlp1=w_mlp1.astype(BF16), w_mlp2=w_mlp2.astype(BF16),
        g_mix_pre=norm_mix_pre[:, None, :], g_mix_post=norm_mix_post[:, None, :], gn=ret_gn[:, None, :],
        g_mlp_pre=norm_mlp_pre[:, None, :], g_mlp_post=norm_mlp_post[:, None, :])
    dec = jnp.concatenate([ret_decay_fwd, ret_decay_bwd], axis=1).astype(F32)
    tabs = _decay_tables(dec)
    cos_t, sin_t = _rope_tables(max(x_prompt.shape[1], x_sample.shape[1]))
    y_prompt = _trunk(x_prompt, w, tabs, cos_t, sin_t)
    y_sample = _trunk(x_sample, w, tabs, cos_t, sin_t)
    return (y_prompt, y_sample)
```

```python
import functools

import numpy as np
import jax
import jax.numpy as jnp
from jax import lax
from jax.experimental import pallas as pl
from jax.experimental.pallas import tpu as pltpu

D_MODEL = 1024
DEPTH = 4
HEADS = 8
DK = 64
DV = 128
PAIRS = HEADS // 2
D_QK = HEADS * DK
CHUNK = 128
ROPE_BASE = 10000.0
POOL_WINDOWS = (2, 4, 8, 16)
POOL_GW = 256
D_FF = 4 * D_MODEL
EPS = 1e-6
QK_SCALE = DK ** -0.5

OFF_U, OFF_V, OFF_A, OFF_SGP, OFF_Q = 0, 1024, 2048, 3072, 4096
D_PROJ = 4608
W_V, W_G, W_U, W_GR, W_GP = 1024, 2048, 3072, 4096, 5120
HALO = 16
POOL_SUB = 128
POOL_WIN = 256
BAND_INTERIOR, BAND_FIRST, BAND_LAST = 0, 1, 2

TB_IN = 512
TB_MIX = 512
WOUT_ROWS, WOUT_COLS = 256, 256
TB_MLP = 512
MLP_SPLIT = 1
VMEM_LIMIT_BYTES = 56 * 1024 * 1024

F32 = jnp.float32
BF16 = jnp.bfloat16


def _rms(x, g):
    return x * lax.rsqrt(jnp.mean(x * x, axis=-1, keepdims=True) + EPS) * g


def _zero_like_row(row):
    bits = lax.bitcast_convert_type(row, jnp.uint32)
    return lax.bitcast_convert_type((bits >> 16) >> 16, F32)


def _log_sigmoid(d):
    return jnp.minimum(d, 0.0) - jnp.log1p(jnp.exp(-jnp.abs(d)))


def _params():
    return pltpu.CompilerParams(dimension_semantics=("arbitrary",), vmem_limit_bytes=VMEM_LIMIT_BYTES)


def _tables_kernel(dec_ref, m_ref, xif_ref, xib_ref, zft_ref, zbt_ref, dcf_ref, dcb_ref):
    l = pl.program_id(0)
    c = CHUNK
    df = [dec_ref[l, h] for h in range(HEADS)]
    db = [dec_ref[l, HEADS + h] for h in range(HEADS)]

    ii = lax.broadcasted_iota(jnp.int32, (c, c), 0)
    jj = lax.broadcasted_iota(jnp.int32, (c, c), 1)
    dist = jnp.abs(ii - jj).astype(F32)
    for h in range(HEADS):
        dsel = jnp.where(ii >= jj, jnp.full((c, c), df[h], F32), jnp.full((c, c), db[h], F32))
        m_ref[0, h // 2, :, (h % 2) * c:(h % 2 + 1) * c] = jnp.exp(dist * _log_sigmoid(dsel))

    rows = xif_ref.shape[1]
    lane = lax.broadcasted_iota(jnp.int32, (rows, D_QK), 1)
    head = 2 * (lane >> 7) + ((lane >> 5) & 1)
    idx = (lax.broadcasted_iota(jnp.int32, (rows, D_QK), 0) & (c - 1)).astype(F32)
    sel_f = jnp.full((rows, D_QK), df[0], F32)
    sel_b = jnp.full((rows, D_QK), db[0], F32)
    for h in range(1, HEADS):
        sel_f = jnp.where(head == h, df[h], sel_f)
        sel_b = jnp.where(head == h, db[h], sel_b)
    xif_ref[0] = jnp.exp((idx + 1.0) * _log_sigmoid(sel_f))
    xib_ref[0] = jnp.exp((c - idx) * _log_sigmoid(sel_b))

    rl = lax.broadcasted_iota(jnp.int32, (2 * DK, c), 0)
    odd = ((rl >> 5) & 1) == 1
    jc = lax.broadcasted_iota(jnp.int32, (2 * DK, c), 1).astype(F32)
    for p in range(PAIRS):
        lg_f = _log_sigmoid(jnp.where(odd, df[2 * p + 1], df[2 * p]))
        lg_b = _log_sigmoid(jnp.where(odd, db[2 * p + 1], db[2 * p]))
        zft_ref[0, p] = jnp.exp((c - 1.0 - jc) * lg_f)
        zbt_ref[0, p] = jnp.exp(jc * lg_b)
        dcf_ref[0, p] = jnp.exp(float(c) * lg_f)
        dcb_ref[0, p] = jnp.exp(float(c) * lg_b)


def _decay_tables(dec):
    c = CHUNK
    pair_tab = jax.ShapeDtypeStruct((DEPTH, PAIRS, 2 * DK, c), F32)
    pair_spec = pl.BlockSpec((1, PAIRS, 2 * DK, c), lambda l: (l, 0, 0, 0))
    lane_tab = jax.ShapeDtypeStruct((DEPTH, c, D_QK), F32)
    lane_spec = pl.BlockSpec((1, c, D_QK), lambda l: (l, 0, 0))
    return pl.pallas_call(
        _tables_kernel,
        grid=(DEPTH,),
        in_specs=[pl.BlockSpec(memory_space=pltpu.SMEM)],
        out_specs=(pl.BlockSpec((1, PAIRS, c, 2 * c), lambda l: (l, 0, 0, 0)),
                   lane_spec, lane_spec, pair_spec, pair_spec, pair_spec, pair_spec),
        out_shape=(jax.ShapeDtypeStruct((DEPTH, PAIRS, c, 2 * c), F32),
                   lane_tab, lane_tab, pair_tab, pair_tab, pair_tab, pair_tab),
        compiler_params=_params(),
    )(dec)


def _fold_pool_kernel(wu_ref, pw_ref, ps_ref, out_ref):
    prod = jnp.dot(wu_ref[...], pw_ref[...], preferred_element_type=F32, precision=lax.Precision.HIGHEST)
    out_ref[...] = (prod * ps_ref[...]).astype(out_ref.dtype)


def _fold_pool(w_in, pool_w, pool_scale):
    n_groups = len(POOL_WINDOWS)
    u_blk = W_U // POOL_GW
    return pl.pallas_call(
        _fold_pool_kernel,
        grid=(DEPTH, n_groups),
        in_specs=[
            pl.BlockSpec((None, D_MODEL, POOL_GW), lambda l, g: (l, 0, u_blk + g)),
            pl.BlockSpec((None, None, POOL_GW, POOL_GW), lambda l, g: (l, g, 0, 0)),
            pl.BlockSpec((None, 1, POOL_GW), lambda l, g: (l, 0, g)),
        ],
        out_specs=pl.BlockSpec((None, D_MODEL, POOL_GW), lambda l, g: (l, 0, g)),
        out_shape=jax.ShapeDtypeStruct((DEPTH, D_MODEL, D_MODEL), BF16),
        compiler_params=pltpu.CompilerParams(dimension_semantics=("arbitrary", "arbitrary"),
                                             vmem_limit_bytes=VMEM_LIMIT_BYTES),
    )(w_in, pool_w, pool_scale[:, None, :])


def _inproj_kernel(seq_len, x_ref, g_ref, wqk_ref, w_ref, wu_ref, cos_ref, sin_ref,
                   zft_ref, dcf_ref, gn_ref, proj_ref, kt_ref, sf_ref, k_scr, stf):
    tb = x_ref.shape[0]

    @pl.when(lax.rem(pl.program_id(0) * tb, seq_len) == 0)
    def _():
        stf[...] = jnp.zeros_like(stf)

    h = _rms(x_ref[...], g_ref[...]).astype(BF16)
    cos = cos_ref[...]
    sin = sin_ref[...]

    qk = jnp.dot(h, wqk_ref[...], preferred_element_type=F32)
    for p in range(PAIRS):
        lo = p * 128
        qp = qk[:, lo:lo + 128]
        qr = (qp * cos + pltpu.roll(qp, 64, 1) * sin) * QK_SCALE
        proj_ref[:, OFF_Q + lo:OFF_Q + lo + 128] = qr.astype(BF16)
        kp = qk[:, D_QK + lo:D_QK + lo + 128]
        kr = kp * cos + pltpu.roll(kp, 64, 1) * sin
        k_scr[:, lo:lo + 128] = kr

    def section(w_off):
        return jnp.dot(h, w_ref[:, w_off:w_off + D_MODEL], preferred_element_type=F32)

    v = section(W_V).astype(BF16)
    proj_ref[:, OFF_V:OFF_V + D_MODEL] = v

    rl = lax.broadcasted_iota(jnp.int32, (2 * DK, DV), 0)
    even_row = ((rl >> 5) & 1) == 0

    def summaries(c):
        r0 = c * CHUNK
        for p in range(PAIRS):
            kt = k_scr[r0:r0 + CHUNK, p * 128:(p + 1) * 128].T
            kt_ref[c, p] = kt.astype(BF16)
            vp = v[r0:r0 + CHUNK, 2 * p * DV:(2 * p + 2) * DV]
            kv = jnp.dot((kt * zft_ref[p]).astype(BF16), vp, preferred_element_type=F32)
            cur = stf[p]
            sf_ref[c, p] = cur.astype(BF16)
            stf[p] = cur * dcf_ref[p] + jnp.where(even_row, kv[:, 0:DV], kv[:, DV:2 * DV])

    g = section(W_G)
    a_gate = g * jax.nn.sigmoid(g) * jax.nn.sigmoid(section(W_GR)) * gn_ref[...]
    proj_ref[:, OFF_A:OFF_A + D_MODEL] = a_gate.astype(BF16)
    proj_ref[:, OFF_U:OFF_U + D_MODEL] = jnp.dot(h, wu_ref[...], preferred_element_type=F32).astype(BF16)
    proj_ref[:, OFF_SGP:OFF_SGP + D_MODEL] = jax.nn.sigmoid(section(W_GP)).astype(BF16)
    for c in range(tb // CHUNK):
        summaries(c)


def _inproj(x, layer, w, tabs, cos_t, sin_t, seq_len):
    t = x.shape[0]
    tb = TB_IN
    nb = t // tb
    nb_seq = seq_len // tb
    cpb = tb // CHUNK
    zft, dcf = tabs
    row_l = pl.BlockSpec((None, 1, D_MODEL), lambda i: (layer, 0, 0))
    pair_l = pl.BlockSpec((None, PAIRS, 2 * DK, CHUNK), lambda i: (layer, 0, 0, 0))
    rope_blk = pl.BlockSpec((tb, 128), lambda i: (i % nb_seq, 0))
    state_blk = pl.BlockSpec((cpb, PAIRS, 2 * DK, DV), lambda i: (i, 0, 0, 0))
    return pl.pallas_call(
        functools.partial(_inproj_kernel, seq_len),
        grid=(nb,),
        in_specs=[
            pl.BlockSpec((tb, D_MODEL), lambda i: (i, 0)),
            row_l,
            pl.BlockSpec((None,) + w["w_qk"].shape[1:], lambda i: (layer, 0, 0), pipeline_mode=pl.Buffered(1)),
            pl.BlockSpec((None,) + w["w_in"].shape[1:], lambda i: (layer, 0, 0), pipeline_mode=pl.Buffered(1)),
            pl.BlockSpec((None, D_MODEL, D_MODEL), lambda i: (layer, 0, 0), pipeline_mode=pl.Buffered(1)),
            rope_blk, rope_blk,
            pair_l, pair_l,
            row_l,
        ],
        out_specs=(pl.BlockSpec((tb, D_PROJ), lambda i: (i, 0)), state_blk, state_blk),
        out_shape=(
            jax.ShapeDtypeStruct((t, D_PROJ), BF16),
            jax.ShapeDtypeStruct((t // CHUNK, PAIRS, 2 * DK, CHUNK), BF16),
            jax.ShapeDtypeStruct((t // CHUNK, PAIRS, 2 * DK, DV), BF16),
        ),
        scratch_shapes=[pltpu.VMEM((tb, D_QK), F32), pltpu.VMEM((PAIRS, 2 * DK, DV), F32)],
        compiler_params=_params(),
    )(x, w["g_mix_pre"], w["w_qk"], w["w_in"], w["w_u"], cos_t, sin_t, zft, dcf, w["gn"])


def _mixer_kernel(seq_len, proj_ref, uprev_ref, unext_ref, kt_ref, sf_ref, xif_ref, xib_ref, zbt_ref, dcb_ref,
                  m_ref, band_ref, wout_ref, out_ref, yr_scr, yp_scr, u_scr, m_scr, stb):
    tb = proj_ref.shape[0]
    c = CHUNK
    step = pl.program_id(0)
    nblk = pl.num_programs(0) - 1
    blk = nblk - 1 - jnp.minimum(step, nblk - 1)
    s0 = lax.rem(blk * tb, seq_len)
    slot = lax.rem(step, 2)

    @pl.when(step == 0)
    def _():
        m_scr[1] = jnp.zeros(m_scr.shape[1:], BF16)

    @pl.when(s0 + tb == seq_len)
    def _():
        stb[...] = jnp.zeros_like(stb)

    def wout_piece(idx):
        rh, nt = divmod(idx, D_MODEL // WOUT_COLS)
        rows = slice(rh * WOUT_ROWS, (rh + 1) * WOUT_ROWS)
        cols = slice(nt * WOUT_COLS, (nt + 1) * WOUT_COLS)
        out_ref[rows, cols] = jnp.dot(m_scr[1 - slot, rows, :], wout_ref[:, cols], preferred_element_type=F32)

    keep_prev = jnp.where(s0 > 0, 1.0, 0.0)
    keep_next = jnp.where(s0 + tb < seq_len, 1.0, 0.0)
    u_scr[0:HALO, :] = (uprev_ref[...].astype(F32) * keep_prev).astype(BF16)
    u_scr[HALO:HALO + tb, :] = proj_ref[:, OFF_U:OFF_U + D_MODEL]
    u_scr[HALO + tb:2 * HALO + tb, :] = (unext_ref[...].astype(F32) * keep_next).astype(BF16)
    u_scr[2 * HALO + tb:, :] = jnp.zeros((u_scr.shape[0] - 2 * HALO - tb, D_MODEL), BF16)
    nsub = tb // POOL_SUB
    first_var = jnp.where(s0 == 0, BAND_FIRST, BAND_INTERIOR)
    last_var = jnp.where(s0 + tb == seq_len, BAND_LAST, BAND_INTERIOR)

    def pool_group(gi):
        lo = gi * POOL_GW
        for sb in range(nsub):
            r0 = sb * POOL_SUB
            var_idx = first_var if sb == 0 else (last_var if sb == nsub - 1 else BAND_INTERIOR)
            yp_scr[r0:r0 + POOL_SUB, lo:lo + POOL_GW] = jnp.dot(
                band_ref[var_idx, gi], u_scr[r0:r0 + POOL_WIN, lo:lo + POOL_GW], preferred_element_type=F32)

    row = lax.broadcasted_iota(jnp.int32, (2 * DK, DV), 0)
    even_row = ((row >> 5) & 1) == 0
    row_even = even_row.astype(F32)
    s_mask = (row_even.astype(BF16), (1.0 - row_even).astype(BF16))
    zeros_v = jnp.zeros((c, DV), BF16)
    order = [(ci, p) for ci in reversed(range(tb // c)) for p in range(PAIRS)]

    def expand(w):
        return jnp.concatenate([w * s_mask[0], w * s_mask[1]], axis=1)

    def early(j):
        ci, p = order[j]
        rows = slice(ci * c, (ci + 1) * c)
        kt = kt_ref[ci, p]
        q = proj_ref[rows, OFF_Q + p * 128:OFF_Q + (p + 1) * 128]
        s = jnp.dot(q, expand(kt), preferred_element_type=F32)
        pm = (s * m_ref[p]).astype(BF16)
        vp = proj_ref[rows, OFF_V + 2 * p * DV:OFF_V + (2 * p + 2) * DV]
        kv = jnp.dot((kt.astype(F32) * zbt_ref[p]).astype(BF16), vp, preferred_element_type=F32)
        qf32 = q.astype(F32)
        lanes = slice(p * 128, (p + 1) * 128)
        q_cross = jnp.concatenate([(qf32 * xif_ref[:, lanes]).astype(BF16),
                                   (qf32 * xib_ref[:, lanes]).astype(BF16)], axis=1)
        return pm, jnp.where(even_row, kv[:, 0:DV], kv[:, DV:2 * DV]), q_cross

    def values(j, early_out):
        pm, kvb, q_cross = early_out
        ci, p = order[j]
        rows = slice(ci * c, (ci + 1) * c)
        lo = p * 128
        v0 = proj_ref[rows, OFF_V + 2 * lo:OFF_V + 2 * lo + DV]
        v1 = proj_ref[rows, OFF_V + 2 * lo + DV:OFF_V + 2 * lo + 2 * DV]
        cur_b = stb[p]
        stb[p] = cur_b * dcb_ref[p] + kvb
        rhs = jnp.concatenate(
            [jnp.concatenate([v0, zeros_v], axis=1),
             jnp.concatenate([zeros_v, v1], axis=1),
             expand(sf_ref[ci, p]),
             expand(cur_b.astype(BF16))], axis=0)
        o2 = jnp.dot(jnp.concatenate([pm, q_cross], axis=1), rhs, preferred_element_type=F32)
        for hh in range(2):
            hl = (2 * p + hh) * DV
            o = o2[:, hh * DV:(hh + 1) * DV]
            mu = jnp.mean(o, axis=-1, keepdims=True)
            d = o - mu
            var = jnp.mean(d * d, axis=-1, keepdims=True)
            a = proj_ref[rows, OFF_A + hl:OFF_A + hl + DV].astype(F32)
            yr_scr[rows, hl:hl + DV] = a * (d * lax.rsqrt(var + EPS))
        if p == PAIRS - 1:
            sgp = proj_ref[rows, OFF_SGP:OFF_SGP + D_MODEL].astype(F32)
            m_scr[slot, rows, :] = (yr_scr[rows, :] + sgp * yp_scr[rows, :]).astype(BF16)

    n_wout = (tb // WOUT_ROWS) * (D_MODEL // WOUT_COLS)
    fillers = {0: lambda: pool_group(0), 1: lambda: pool_group(1), 2: lambda: pool_group(2),
               3: lambda: pool_group(3)}
    for k in range(1, n_wout - 1):
        fillers[2 + 2 * k] = functools.partial(wout_piece, k)
    assert max(fillers) < len(order)

    wout_piece(0)
    pending = {0: early(0), 1: early(1)}
    for j in range(len(order)):
        if j in fillers:
            fillers[j]()
        values(j, pending.pop(j))
        if j + 2 < len(order):
            pending[j + 2] = early(j + 2)
    wout_piece(n_wout - 1)


def _pool_bands():
    i = np.arange(POOL_SUB)[:, None]
    j = np.arange(POOL_WIN)[None, :] - HALO
    out = np.zeros((3, len(POOL_WINDOWS), POOL_SUB, POOL_WIN), np.float64)
    for gi, w in enumerate(POOL_WINDOWS):
        half = w // 2
        inside = ((j - i >= -half) & (j - i <= half - 1)).astype(np.float64)
        token = (j == i).astype(np.float64)
        counts = {
            BAND_INTERIOR: np.full((POOL_SUB, 1), float(w)),
            BAND_FIRST: (i + half) - np.maximum(i - half, 0),
            BAND_LAST: np.minimum(i + half, POOL_SUB) - (i - half),
        }
        for var, cnt in counts.items():
            out[var, gi] = inside / cnt - token
    return out.astype(np.float32)


def _mixer(proj, kt, sf, layer, w, tabs, seq_len):
    t = proj.shape[0]
    tb = TB_MIX
    nb = t // tb
    cpb = tb // CHUNK
    hb = tb // HALO
    n_halo = t // HALO
    u_col = OFF_U // D_MODEL
    xif, xib, zbt, dcb, m_tab = tabs
    band = jnp.asarray(_pool_bands(), BF16)
    rev = lambda i: nb - 1 - jnp.minimum(i, nb - 1)
    rev_out = lambda i: nb - 1 - jnp.maximum(i - 1, 0)
    lay3 = lambda i: (layer, 0, 0)
    lay4 = lambda i: (layer, 0, 0, 0)
    state_blk = pl.BlockSpec((cpb, PAIRS, 2 * DK, DV), lambda i: (rev(i), 0, 0, 0))
    return pl.pallas_call(
        functools.partial(_mixer_kernel, seq_len),
        grid=(nb + 1,),
        in_specs=[
            pl.BlockSpec((tb, D_PROJ), lambda i: (rev(i), 0)),
            pl.BlockSpec((HALO, D_MODEL), lambda i: (jnp.maximum(rev(i) * hb - 1, 0), u_col)),
            pl.BlockSpec((HALO, D_MODEL), lambda i: (jnp.minimum((rev(i) + 1) * hb, n_halo - 1), u_col)),
            state_blk, state_blk,
            pl.BlockSpec((None, CHUNK, D_QK), lay3),
            pl.BlockSpec((None, CHUNK, D_QK), lay3),
            pl.BlockSpec((None, PAIRS, 2 * DK, CHUNK), lay4),
            pl.BlockSpec((None, PAIRS, 2 * DK, DV), lay4),
            pl.BlockSpec((None, PAIRS, CHUNK, 2 * CHUNK), lay4),
            pl.BlockSpec(band.shape, lambda i: (0, 0, 0, 0)),
            pl.BlockSpec((None, D_MODEL, D_MODEL), lay3),
        ],
        out_specs=pl.BlockSpec((tb, D_MODEL), lambda i: (rev_out(i), 0)),
        out_shape=jax.ShapeDtypeStruct((t, D_MODEL), F32),
        scratch_shapes=[
            pltpu.VMEM((tb, D_MODEL), F32),
            pltpu.VMEM((tb, D_MODEL), F32),
            pltpu.VMEM((tb - POOL_SUB + POOL_WIN, D_MODEL), BF16),
            pltpu.VMEM((2, tb, D_MODEL), BF16),
            pltpu.VMEM((PAIRS, 2 * DK, DV), F32),
        ],
        compiler_params=_params(),
    )(proj, proj, proj, kt, sf, xif, xib, zbt, dcb, m_tab, band, w["w_out"])


def _mlp_body(x_ref, z_ref, gmix_ref, gpre_ref, w1_ref, w2_ref, gpost_ref, out_ref, x1_rd, h_rd, x1_wr, h_wr):
    tb = x_ref.shape[0]
    n_ff = D_FF // D_MODEL
    rows_per = tb // (2 * n_ff)

    def norms(k):
        rows = slice(k * rows_per, (k + 1) * rows_per)
        x1 = x_ref[rows, :] + _rms(z_ref[rows, :], gmix_ref[...])
        x1_wr[rows, :] = x1
        h_wr[rows, :] = _rms(x1, gpre_ref[...]).astype(BF16)
        return _zero_like_row(x1[0:1, :])

    h = h_rd[...]
    acc = jnp.zeros((tb, D_MODEL), F32)
    for j in range(n_ff):
        lo = j * D_MODEL
        floor = jnp.maximum(norms(2 * j), norms(2 * j + 1))
        f = jnp.dot(h, w1_ref[:, lo:lo + D_MODEL], preferred_element_type=F32)
        f = jnp.square(jnp.maximum(f, floor)).astype(BF16)
        acc = acc + jnp.dot(f, w2_ref[lo:lo + D_MODEL, :], preferred_element_type=F32)
    out_ref[...] = x1_rd[...] + _rms(acc, gpost_ref[...])


def _mlp_kernel(x_ref, z_ref, gmix_ref, gpre_ref, w1_ref, w2_ref, gpost_ref, out_ref, x1_a, h_a, x1_b, h_b):
    step = pl.program_id(0)
    args = (x_ref, z_ref, gmix_ref, gpre_ref, w1_ref, w2_ref, gpost_ref, out_ref)

    @pl.when(step == 0)
    def _():
        x1_b[...] = jnp.zeros_like(x1_b)
        h_b[...] = jnp.zeros_like(h_b)

    @pl.when(lax.rem(step, 2) == 0)
    def _():
        _mlp_body(*args, x1_b, h_b, x1_a, h_a)

    @pl.when(lax.rem(step, 2) == 1)
    def _():
        _mlp_body(*args, x1_a, h_a, x1_b, h_b)


def _mlp(x, z, layer, w):
    t = x.shape[0]
    tb = TB_MLP
    nb = t // tb
    lay3 = lambda i: (layer, 0, 0)
    row_l = pl.BlockSpec((None, 1, D_MODEL), lay3)
    cur = lambda i: (jnp.minimum(i, nb - 1), 0)
    return pl.pallas_call(
        _mlp_kernel,
        grid=(nb + 1,),
        in_specs=[
            pl.BlockSpec((tb, D_MODEL), cur),
            pl.BlockSpec((tb, D_MODEL), cur),
            row_l, row_l,
            pl.BlockSpec((None, D_MODEL, D_FF), lay3, pipeline_mode=pl.Buffered(1)),
            pl.BlockSpec((None, D_FF, D_MODEL), lay3, pipeline_mode=pl.Buffered(1)),
            row_l,
        ],
        out_specs=pl.BlockSpec((tb, D_MODEL), lambda i: (jnp.maximum(i - 1, 0), 0)),
        out_shape=jax.ShapeDtypeStruct((t, D_MODEL), F32),
        scratch_shapes=[pltpu.VMEM((tb, D_MODEL), F32), pltpu.VMEM((tb, D_MODEL), BF16),
                        pltpu.VMEM((tb, D_MODEL), F32), pltpu.VMEM((tb, D_MODEL), BF16)],
        compiler_params=_params(),
    )(x, z, w["g_mix_post"], w["g_mlp_pre"], w["w_mlp1"], w["w_mlp2"], w["g_mlp_post"])


def _qk_column_order():
    n = np.arange(D_QK)
    pair, l = n // 128, n % 128
    head = 2 * pair + (l // 32) % 2
    return head * DK + (l // 64) * (DK // 2) + l % 32


def _rope_tables(seq_len):
    half = DK // 2
    inv = ROPE_BASE ** (-jnp.arange(half, dtype=F32) / half)
    ang = jnp.arange(seq_len, dtype=F32)[:, None] * inv[None, :]
    cos, sin = jnp.cos(ang), jnp.sin(ang)
    return jnp.tile(cos, (1, 4)), jnp.concatenate([-sin, -sin, sin, sin], axis=1)


def _trunk(x, w, tabs, cos_t, sin_t):
    batch, seq_len, _ = x.shape
    xt = x.reshape(batch * seq_len, D_MODEL)
    m_tab, xif, xib, zft, zbt, dcf, dcb = tabs
    for layer in range(DEPTH):
        proj, kt, sf = _inproj(xt, layer, w, (zft, dcf), cos_t, sin_t, seq_len)
        z = _mixer(proj, kt, sf, layer, w, (xif, xib, zbt, dcb, m_tab), seq_len)
        xt = _mlp(xt, z, layer, w)
    return xt.reshape(batch, seq_len, D_MODEL)


def kernel(x_prompt, x_sample, norm_mix_pre, norm_mix_post, w_in, ret_decay_fwd, ret_decay_bwd, ret_gn,
           pool_w, pool_scale, w_out, norm_mlp_pre, norm_mlp_post, w_mlp1, w_mlp2):
    order = _qk_column_order()
    qk_cols = np.concatenate([order, D_QK + order])
    w = dict(
        w_qk=w_in[:, :, :2 * D_QK][:, :, qk_cols].astype(BF16), w_in=w_in.astype(BF16),
        w_u=_fold_pool(w_in, pool_w, pool_scale), w_out=w_out.astype(BF16),
        w_mlp1=w_mlp1.astype(BF16), w_mlp2=w_mlp2.astype(BF16),
        g_mix_pre=norm_mix_pre[:, None, :], g_mix_post=norm_mix_post[:, None, :], gn=ret_gn[:, None, :],
        g_mlp_pre=norm_mlp_pre[:, None, :], g_mlp_post=norm_mlp_post[:, None, :])
    dec = jnp.concatenate([ret_decay_fwd, ret_decay_bwd], axis=1).astype(F32)
    tabs = _decay_tables(dec)
    cos_t, sin_t = _rope_tables(max(x_prompt.shape[1], x_sample.shape[1]))
    y_prompt = _trunk(x_prompt, w, tabs, cos_t, sin_t)
    y_sample = _trunk(x_sample, w, tabs, cos_t, sin_t)
    return (y_prompt, y_sample)
```

```python
import functools

import numpy as np
import jax
import jax.numpy as jnp
from jax import lax
from jax.experimental import pallas as pl
from jax.experimental.pallas import tpu as pltpu

D_MODEL = 1024
DEPTH = 4
HEADS = 8
DK = 64
DV = 128
PAIRS = HEADS // 2
D_QK = HEADS * DK
CHUNK = 128
ROPE_BASE = 10000.0
POOL_WINDOWS = (2, 4, 8, 16)
POOL_GW = 256
D_FF = 4 * D_MODEL
EPS = 1e-6
QK_SCALE = DK ** -0.5

OFF_U, OFF_V, OFF_A, OFF_SGP, OFF_Q = 0, 1024, 2048, 3072, 4096
D_PROJ = 4608
W_V, W_G, W_U, W_GR, W_GP = 1024, 2048, 3072, 4096, 5120
HALO = 16
POOL_SUB = 128
POOL_WIN = 256
BAND_INTERIOR, BAND_FIRST, BAND_LAST = 0, 1, 2

TB_IN = 512
TB_MIX = 512
WOUT_ROWS, WOUT_COLS = 256, 256
TB_MLP = 512
VMEM_LIMIT_BYTES = 56 * 1024 * 1024

F32 = jnp.float32
BF16 = jnp.bfloat16


def _rms(x, g):
    return x * lax.rsqrt(jnp.mean(x * x, axis=-1, keepdims=True) + EPS) * g


def _zero_like_row(row):
    bits = lax.bitcast_convert_type(row, jnp.uint32)
    return lax.bitcast_convert_type((bits >> 16) >> 16, F32)


def _log_sigmoid(d):
    return jnp.minimum(d, 0.0) - jnp.log1p(jnp.exp(-jnp.abs(d)))


def _params():
    return pltpu.CompilerParams(dimension_semantics=("arbitrary",), vmem_limit_bytes=VMEM_LIMIT_BYTES)


def _tables_kernel(dec_ref, m_ref, xif_ref, xib_ref, zft_ref, zbt_ref, dcf_ref, dcb_ref):
    l = pl.program_id(0)
    c = CHUNK
    df = [dec_ref[l, h] for h in range(HEADS)]
    db = [dec_ref[l, HEADS + h] for h in range(HEADS)]

    ii = lax.broadcasted_iota(jnp.int32, (c, c), 0)
    jj = lax.broadcasted_iota(jnp.int32, (c, c), 1)
    dist = jnp.abs(ii - jj).astype(F32)
    for h in range(HEADS):
        dsel = jnp.where(ii >= jj, jnp.full((c, c), df[h], F32), jnp.full((c, c), db[h], F32))
        m_ref[0, h // 2, :, (h % 2) * c:(h % 2 + 1) * c] = jnp.exp(dist * _log_sigmoid(dsel))

    rows = xif_ref.shape[1]
    lane = lax.broadcasted_iota(jnp.int32, (rows, D_QK), 1)
    head = 2 * (lane >> 7) + ((lane >> 5) & 1)
    idx = (lax.broadcasted_iota(jnp.int32, (rows, D_QK), 0) & (c - 1)).astype(F32)
    sel_f = jnp.full((rows, D_QK), df[0], F32)
    sel_b = jnp.full((rows, D_QK), db[0], F32)
    for h in range(1, HEADS):
        sel_f = jnp.where(head == h, df[h], sel_f)
        sel_b = jnp.where(head == h, db[h], sel_b)
    xif_ref[0] = jnp.exp((idx + 1.0) * _log_sigmoid(sel_f))
    xib_ref[0] = jnp.exp((c - idx) * _log_sigmoid(sel_b))

    rl = lax.broadcasted_iota(jnp.int32, (2 * DK, c), 0)
    odd = ((rl >> 5) & 1) == 1
    jc = lax.broadcasted_iota(jnp.int32, (2 * DK, c), 1).astype(F32)
    for p in range(PAIRS):
        lg_f = _log_sigmoid(jnp.where(odd, df[2 * p + 1], df[2 * p]))
        lg_b = _log_sigmoid(jnp.where(odd, db[2 * p + 1], db[2 * p]))
        zft_ref[0, p] = jnp.exp((c - 1.0 - jc) * lg_f)
        zbt_ref[0, p] = jnp.exp(jc * lg_b)
        dcf_ref[0, p] = jnp.exp(float(c) * lg_f)
        dcb_ref[0, p] = jnp.exp(float(c) * lg_b)


def _decay_tables(dec):
    c = CHUNK
    pair_tab = jax.ShapeDtypeStruct((DEPTH, PAIRS, 2 * DK, c), F32)
    pair_spec = pl.BlockSpec((1, PAIRS, 2 * DK, c), lambda l: (l, 0, 0, 0))
    lane_tab = jax.ShapeDtypeStruct((DEPTH, c, D_QK), F32)
    lane_spec = pl.BlockSpec((1, c, D_QK), lambda l: (l, 0, 0))
    return pl.pallas_call(
        _tables_kernel,
        grid=(DEPTH,),
        in_specs=[pl.BlockSpec(memory_space=pltpu.SMEM)],
        out_specs=(pl.BlockSpec((1, PAIRS, c, 2 * c), lambda l: (l, 0, 0, 0)),
                   lane_spec, lane_spec, pair_spec, pair_spec, pair_spec, pair_spec),
        out_shape=(jax.ShapeDtypeStruct((DEPTH, PAIRS, c, 2 * c), F32),
                   lane_tab, lane_tab, pair_tab, pair_tab, pair_tab, pair_tab),
        compiler_params=_params(),
    )(dec)


def _fold_pool_kernel(wu_ref, pw_ref, ps_ref, out_ref):
    prod = jnp.dot(wu_ref[...], pw_ref[...], preferred_element_type=F32, precision=lax.Precision.HIGHEST)
    out_ref[...] = (prod * ps_ref[...]).astype(out_ref.dtype)


def _fold_pool(w_in, pool_w, pool_scale):
    n_groups = len(POOL_WINDOWS)
    u_blk = W_U // POOL_GW
    return pl.pallas_call(
        _fold_pool_kernel,
        grid=(DEPTH, n_groups),
        in_specs=[
            pl.BlockSpec((None, D_MODEL, POOL_GW), lambda l, g: (l, 0, u_blk + g)),
            pl.BlockSpec((None, None, POOL_GW, POOL_GW), lambda l, g: (l, g, 0, 0)),
            pl.BlockSpec((None, 1, POOL_GW), lambda l, g: (l, 0, g)),
        ],
        out_specs=pl.BlockSpec((None, D_MODEL, POOL_GW), lambda l, g: (l, 0, g)),
        out_shape=jax.ShapeDtypeStruct((DEPTH, D_MODEL, D_MODEL), BF16),
        compiler_params=pltpu.CompilerParams(dimension_semantics=("arbitrary", "arbitrary"),
                                             vmem_limit_bytes=VMEM_LIMIT_BYTES),
    )(w_in, pool_w, pool_scale[:, None, :])


def _inproj_kernel(seq_len, x_ref, g_ref, wqk_ref, w_ref, wu_ref, cos_ref, sin_ref,
                   zft_ref, dcf_ref, gn_ref, proj_ref, kt_ref, sf_ref, k_scr, stf):
    tb = x_ref.shape[0]

    @pl.when(lax.rem(pl.program_id(0) * tb, seq_len) == 0)
    def _():
        stf[...] = jnp.zeros_like(stf)

    half = tb // 2
    h_top = _rms(x_ref[0:half, :], g_ref[...]).astype(BF16)
    qk_top = jnp.dot(h_top, wqk_ref[...], preferred_element_type=F32)
    h_bot = _rms(x_ref[half:tb, :], g_ref[...]).astype(BF16)
    qk_bot = jnp.dot(h_bot, wqk_ref[...], preferred_element_type=F32)
    h = jnp.concatenate([h_top, h_bot], axis=0)
    qk = jnp.concatenate([qk_top, qk_bot], axis=0)
    cos = cos_ref[...]
    sin = sin_ref[...]

    for p in range(PAIRS):
        lo = p * 128
        qp = qk[:, lo:lo + 128]
        qr = (qp * cos + pltpu.roll(qp, 64, 1) * sin) * QK_SCALE
        proj_ref[:, OFF_Q + lo:OFF_Q + lo + 128] = qr.astype(BF16)
        kp = qk[:, D_QK + lo:D_QK + lo + 128]
        kr = kp * cos + pltpu.roll(kp, 64, 1) * sin
        k_scr[:, lo:lo + 128] = kr

    def section(w_off):
        return jnp.dot(h, w_ref[:, w_off:w_off + D_MODEL], preferred_element_type=F32)

    v = section(W_V).astype(BF16)
    proj_ref[:, OFF_V:OFF_V + D_MODEL] = v

    rl = lax.broadcasted_iota(jnp.int32, (2 * DK, DV), 0)
    even_row = ((rl >> 5) & 1) == 0

    def summaries(c):
        r0 = c * CHUNK
        for p in range(PAIRS):
            kt = k_scr[r0:r0 + CHUNK, p * 128:(p + 1) * 128].T
            kt_ref[c, p] = kt.astype(BF16)
            vp = v[r0:r0 + CHUNK, 2 * p * DV:(2 * p + 2) * DV]
            kv = jnp.dot((kt * zft_ref[p]).astype(BF16), vp, preferred_element_type=F32)
            cur = stf[p]
            sf_ref[c, p] = cur.astype(BF16)
            stf[p] = cur * dcf_ref[p] + jnp.where(even_row, kv[:, 0:DV], kv[:, DV:2 * DV])

    g = section(W_G)
    a_gate = g * jax.nn.sigmoid(g) * jax.nn.sigmoid(section(W_GR)) * gn_ref[...]
    proj_ref[:, OFF_A:OFF_A + D_MODEL] = a_gate.astype(BF16)
    proj_ref[:, OFF_U:OFF_U + D_MODEL] = jnp.dot(h, wu_ref[...], preferred_element_type=F32).astype(BF16)
    proj_ref[:, OFF_SGP:OFF_SGP + D_MODEL] = jax.nn.sigmoid(section(W_GP)).astype(BF16)
    for c in range(tb // CHUNK):
        summaries(c)


def _inproj(x, layer, w, tabs, cos_t, sin_t, seq_len):
    t = x.shape[0]
    tb = TB_IN
    nb = t // tb
    nb_seq = seq_len // tb
    cpb = tb // CHUNK
    zft, dcf = tabs
    row_l = pl.BlockSpec((None, 1, D_MODEL), lambda i: (layer, 0, 0))
    pair_l = pl.BlockSpec((None, PAIRS, 2 * DK, CHUNK), lambda i: (layer, 0, 0, 0))
    rope_blk = pl.BlockSpec((tb, 128), lambda i: (i % nb_seq, 0))
    state_blk = pl.BlockSpec((cpb, PAIRS, 2 * DK, DV), lambda i: (i, 0, 0, 0))
    return pl.pallas_call(
        functools.partial(_inproj_kernel, seq_len),
        grid=(nb,),
        in_specs=[
            pl.BlockSpec((tb, D_MODEL), lambda i: (i, 0)),
            row_l,
            pl.BlockSpec((None,) + w["w_qk"].shape[1:], lambda i: (layer, 0, 0), pipeline_mode=pl.Buffered(1)),
            pl.BlockSpec((None,) + w["w_in"].shape[1:], lambda i: (layer, 0, 0), pipeline_mode=pl.Buffered(1)),
            pl.BlockSpec((None, D_MODEL, D_MODEL), lambda i: (layer, 0, 0), pipeline_mode=pl.Buffered(1)),
            rope_blk, rope_blk,
            pair_l, pair_l,
            row_l,
        ],
        out_specs=(pl.BlockSpec((tb, D_PROJ), lambda i: (i, 0)), state_blk, state_blk),
        out_shape=(
            jax.ShapeDtypeStruct((t, D_PROJ), BF16),
            jax.ShapeDtypeStruct((t // CHUNK, PAIRS, 2 * DK, CHUNK), BF16),
            jax.ShapeDtypeStruct((t // CHUNK, PAIRS, 2 * DK, DV), BF16),
        ),
        scratch_shapes=[pltpu.VMEM((tb, D_QK), F32), pltpu.VMEM((PAIRS, 2 * DK, DV), F32)],
        compiler_params=_params(),
    )(x, w["g_mix_pre"], w["w_qk"], w["w_in"], w["w_u"], cos_t, sin_t, zft, dcf, w["gn"])


def _mixer_kernel(seq_len, proj_ref, uprev_ref, unext_ref, kt_ref, sf_ref, xif_ref, xib_ref, zbt_ref, dcb_ref,
                  m_ref, band_ref, wout_ref, out_ref, yr_scr, yp_scr, u_scr, m_scr, stb):
    tb = proj_ref.shape[0]
    c = CHUNK
    step = pl.program_id(0)
    nblk = pl.num_programs(0) - 1
    blk = nblk - 1 - jnp.minimum(step, nblk - 1)
    s0 = lax.rem(blk * tb, seq_len)
    slot = lax.rem(step, 2)

    @pl.when(step == 0)
    def _():
        m_scr[1] = jnp.zeros(m_scr.shape[1:], BF16)

    @pl.when(s0 + tb == seq_len)
    def _():
        stb[...] = jnp.zeros_like(stb)

    def wout_piece(idx):
        rh, nt = divmod(idx, D_MODEL // WOUT_COLS)
        rows = slice(rh * WOUT_ROWS, (rh + 1) * WOUT_ROWS)
        cols = slice(nt * WOUT_COLS, (nt + 1) * WOUT_COLS)
        out_ref[rows, cols] = jnp.dot(m_scr[1 - slot, rows, :], wout_ref[:, cols], preferred_element_type=F32)

    keep_prev = jnp.where(s0 > 0, 1.0, 0.0)
    keep_next = jnp.where(s0 + tb < seq_len, 1.0, 0.0)
    u_scr[0:HALO, :] = (uprev_ref[...].astype(F32) * keep_prev).astype(BF16)
    u_scr[HALO:HALO + tb, :] = proj_ref[:, OFF_U:OFF_U + D_MODEL]
    u_scr[HALO + tb:2 * HALO + tb, :] = (unext_ref[...].astype(F32) * keep_next).astype(BF16)
    u_scr[2 * HALO + tb:, :] = jnp.zeros((u_scr.shape[0] - 2 * HALO - tb, D_MODEL), BF16)
    nsub = tb // POOL_SUB
    first_var = jnp.where(s0 == 0, BAND_FIRST, BAND_INTERIOR)
    last_var = jnp.where(s0 + tb == seq_len, BAND_LAST, BAND_INTERIOR)

    def pool_group(gi):
        lo = gi * POOL_GW
        for sb in range(nsub):
            r0 = sb * POOL_SUB
            var_idx = first_var if sb == 0 else (last_var if sb == nsub - 1 else BAND_INTERIOR)
            yp_scr[r0:r0 + POOL_SUB, lo:lo + POOL_GW] = jnp.dot(
                band_ref[var_idx, gi], u_scr[r0:r0 + POOL_WIN, lo:lo + POOL_GW], preferred_element_type=F32)

    row = lax.broadcasted_iota(jnp.int32, (2 * DK, DV), 0)
    even_row = ((row >> 5) & 1) == 0
    row_even = even_row.astype(F32)
    s_mask = (row_even.astype(BF16), (1.0 - row_even).astype(BF16))
    zeros_v = jnp.zeros((c, DV), BF16)
    order = [(ci, p) for ci in reversed(range(tb // c)) for p in range(PAIRS)]

    def expand(w):
        return jnp.concatenate([w * s_mask[0], w * s_mask[1]], axis=1)

    def early(j):
        ci, p = order[j]
        rows = slice(ci * c, (ci + 1) * c)
        kt = kt_ref[ci, p]
        q = proj_ref[rows, OFF_Q + p * 128:OFF_Q + (p + 1) * 128]
        s = jnp.dot(q, expand(kt), preferred_element_type=F32)
        pm = (s * m_ref[p]).astype(BF16)
        vp = proj_ref[rows, OFF_V + 2 * p * DV:OFF_V + (2 * p + 2) * DV]
        kv = jnp.dot((kt.astype(F32) * zbt_ref[p]).astype(BF16), vp, preferred_element_type=F32)
        qf32 = q.astype(F32)
        lanes = slice(p * 128, (p + 1) * 128)
        q_cross = jnp.concatenate([(qf32 * xif_ref[:, lanes]).astype(BF16),
                                   (qf32 * xib_ref[:, lanes]).astype(BF16)], axis=1)
        return pm, jnp.where(even_row, kv[:, 0:DV], kv[:, DV:2 * DV]), q_cross

    def values(j, early_out):
        pm, kvb, q_cross = early_out
        ci, p = order[j]
        rows = slice(ci * c, (ci + 1) * c)
        lo = p * 128
        v0 = proj_ref[rows, OFF_V + 2 * lo:OFF_V + 2 * lo + DV]
        v1 = proj_ref[rows, OFF_V + 2 * lo + DV:OFF_V + 2 * lo + 2 * DV]
        cur_b = stb[p]
        stb[p] = cur_b * dcb_ref[p] + kvb
        rhs = jnp.concatenate(
            [jnp.concatenate([v0, zeros_v], axis=1),
             jnp.concatenate([zeros_v, v1], axis=1),
             expand(sf_ref[ci, p]),
             expand(cur_b.astype(BF16))], axis=0)
        o2 = jnp.dot(jnp.concatenate([pm, q_cross], axis=1), rhs, preferred_element_type=F32)
        for hh in range(2):
            hl = (2 * p + hh) * DV
            o = o2[:, hh * DV:(hh + 1) * DV]
            mu = jnp.mean(o, axis=-1, keepdims=True)
            d = o - mu
            var = jnp.mean(d * d, axis=-1, keepdims=True)
            a = proj_ref[rows, OFF_A + hl:OFF_A + hl + DV].astype(F32)
            yr_scr[rows, hl:hl + DV] = a * (d * lax.rsqrt(var + EPS))
        if p == PAIRS - 1:
            sgp = proj_ref[rows, OFF_SGP:OFF_SGP + D_MODEL].astype(F32)
            m_scr[slot, rows, :] = (yr_scr[rows, :] + sgp * yp_scr[rows, :]).astype(BF16)

    n_wout = (tb // WOUT_ROWS) * (D_MODEL // WOUT_COLS)
    fillers = {0: lambda: pool_group(0), 1: lambda: pool_group(1), 2: lambda: pool_group(2),
               3: lambda: pool_group(3)}
    for k in range(1, n_wout - 1):
        fillers[2 + 2 * k] = functools.partial(wout_piece, k)
    assert max(fillers) < len(order)

    wout_piece(0)
    pending = {0: early(0), 1: early(1)}
    for j in range(len(order)):
        if j in fillers:
            fillers[j]()
        values(j, pending.pop(j))
        if j + 2 < len(order):
            pending[j + 2] = early(j + 2)
    wout_piece(n_wout - 1)


def _pool_bands():
    i = np.arange(POOL_SUB)[:, None]
    j = np.arange(POOL_WIN)[None, :] - HALO
    out = np.zeros((3, len(POOL_WINDOWS), POOL_SUB, POOL_WIN), np.float64)
    for gi, w in enumerate(POOL_WINDOWS):
        half = w // 2
        inside = ((j - i >= -half) & (j - i <= half - 1)).astype(np.float64)
        token = (j == i).astype(np.float64)
        counts = {
            BAND_INTERIOR: np.full((POOL_SUB, 1), float(w)),
            BAND_FIRST: (i + half) - np.maximum(i - half, 0),
            BAND_LAST: np.minimum(i + half, POOL_SUB) - (i - half),
        }
        for var, cnt in counts.items():
            out[var, gi] = inside / cnt - token
    return out.astype(np.float32)


def _mixer(proj, kt, sf, layer, w, tabs, seq_len):
    t = proj.shape[0]
    tb = TB_MIX
    nb = t // tb
    cpb = tb // CHUNK
    hb = tb // HALO
    n_halo = t // HALO
    u_col = OFF_U // D_MODEL
    xif, xib, zbt, dcb, m_tab = tabs
    band = jnp.asarray(_pool_bands(), BF16)
    rev = lambda i: nb - 1 - jnp.minimum(i, nb - 1)
    rev_out = lambda i: nb - 1 - jnp.maximum(i - 1, 0)
    lay3 = lambda i: (layer, 0, 0)
    lay4 = lambda i: (layer, 0, 0, 0)
    state_blk = pl.BlockSpec((cpb, PAIRS, 2 * DK, DV), lambda i: (rev(i), 0, 0, 0))
    return pl.pallas_call(
        functools.partial(_mixer_kernel, seq_len),
        grid=(nb + 1,),
        in_specs=[
            pl.BlockSpec((tb, D_PROJ), lambda i: (rev(i), 0)),
            pl.BlockSpec((HALO, D_MODEL), lambda i: (jnp.maximum(rev(i) * hb - 1, 0), u_col)),
            pl.BlockSpec((HALO, D_MODEL), lambda i: (jnp.minimum((rev(i) + 1) * hb, n_halo - 1), u_col)),
            state_blk, state_blk,
            pl.BlockSpec((None, CHUNK, D_QK), lay3),
            pl.BlockSpec((None, CHUNK, D_QK), lay3),
            pl.BlockSpec((None, PAIRS, 2 * DK, CHUNK), lay4),
            pl.BlockSpec((None, PAIRS, 2 * DK, DV), lay4),
            pl.BlockSpec((None, PAIRS, CHUNK, 2 * CHUNK), lay4),
            pl.BlockSpec(band.shape, lambda i: (0, 0, 0, 0)),
            pl.BlockSpec((None, D_MODEL, D_MODEL), lay3),
        ],
        out_specs=pl.BlockSpec((tb, D_MODEL), lambda i: (rev_out(i), 0)),
        out_shape=jax.ShapeDtypeStruct((t, D_MODEL), F32),
        scratch_shapes=[
            pltpu.VMEM((tb, D_MODEL), F32),
            pltpu.VMEM((tb, D_MODEL), F32),
            pltpu.VMEM((tb - POOL_SUB + POOL_WIN, D_MODEL), BF16),
            pltpu.VMEM((2, tb, D_MODEL), BF16),
            pltpu.VMEM((PAIRS, 2 * DK, DV), F32),
        ],
        compiler_params=_params(),
    )(proj, proj, proj, kt, sf, xif, xib, zbt, dcb, m_tab, band, w["w_out"])


def _mlp_body(x_ref, z_ref, gmix_ref, gpre_ref, w1_ref, w2_ref, gpost_ref, out_ref, x1_rd, h_rd, x1_wr, h_wr):
    tb = x_ref.shape[0]
    n_ff = D_FF // D_MODEL
    rows_per = tb // (2 * n_ff)

    def norms(k):
        rows = slice(k * rows_per, (k + 1) * rows_per)
        x1 = x_ref[rows, :] + _rms(z_ref[rows, :], gmix_ref[...])
        x1_wr[rows, :] = x1
        h_wr[rows, :] = _rms(x1, gpre_ref[...]).astype(BF16)
        return _zero_like_row(x1[0:1, :])

    h = h_rd[...]
    acc = jnp.zeros((tb, D_MODEL), F32)
    for j in range(n_ff):
        lo = j * D_MODEL
        floor = jnp.maximum(norms(2 * j), norms(2 * j + 1))
        f = jnp.dot(h, w1_ref[:, lo:lo + D_MODEL], preferred_element_type=F32)
        f = jnp.square(jnp.maximum(f, floor)).astype(BF16)
        acc = acc + jnp.dot(f, w2_ref[lo:lo + D_MODEL, :], preferred_element_type=F32)
    out_ref[...] = x1_rd[...] + _rms(acc, gpost_ref[...])


def _mlp_kernel(x_ref, z_ref, gmix_ref, gpre_ref, w1_ref, w2_ref, gpost_ref, out_ref, x1_a, h_a, x1_b, h_b):
    step = pl.program_id(0)
    args = (x_ref, z_ref, gmix_ref, gpre_ref, w1_ref, w2_ref, gpost_ref, out_ref)

    @pl.when(step == 0)
    def _():
        x1_b[...] = jnp.zeros_like(x1_b)
        h_b[...] = jnp.zeros_like(h_b)

    @pl.when(lax.rem(step, 2) == 0)
    def _():
        _mlp_body(*args, x1_b, h_b, x1_a, h_a)

    @pl.when(lax.rem(step, 2) == 1)
    def _():
        _mlp_body(*args, x1_a, h_a, x1_b, h_b)


def _mlp(x, z, layer, w):
    t = x.shape[0]
    tb = TB_MLP
    nb = t // tb
    lay3 = lambda i: (layer, 0, 0)
    row_l = pl.BlockSpec((None, 1, D_MODEL), lay3)
    cur = lambda i: (jnp.minimum(i, nb - 1), 0)
    return pl.pallas_call(
        _mlp_kernel,
        grid=(nb + 1,),
        in_specs=[
            pl.BlockSpec((tb, D_MODEL), cur),
            pl.BlockSpec((tb, D_MODEL), cur),
            row_l, row_l,
            pl.BlockSpec((None, D_MODEL, D_FF), lay3, pipeline_mode=pl.Buffered(1)),
            pl.BlockSpec((None, D_FF, D_MODEL), lay3, pipeline_mode=pl.Buffered(1)),
            row_l,
        ],
        out_specs=pl.BlockSpec((tb, D_MODEL), lambda i: (jnp.maximum(i - 1, 0), 0)),
        out_shape=jax.ShapeDtypeStruct((t, D_MODEL), F32),
        scratch_shapes=[pltpu.VMEM((tb, D_MODEL), F32), pltpu.VMEM((tb, D_MODEL), BF16),
                        pltpu.VMEM((tb, D_MODEL), F32), pltpu.VMEM((tb, D_MODEL), BF16)],
        compiler_params=_params(),
    )(x, z, w["g_mix_post"], w["g_mlp_pre"], w["w_mlp1"], w["w_mlp2"], w["g_mlp_post"])


def _qk_column_order():
    n = np.arange(D_QK)
    pair, l = n // 128, n % 128
    head = 2 * pair + (l // 32) % 2
    return head * DK + (l // 64) * (DK // 2) + l % 32


def _rope_tables(seq_len):
    half = DK // 2
    inv = ROPE_BASE ** (-jnp.arange(half, dtype=F32) / half)
    ang = jnp.arange(seq_len, dtype=F32)[:, None] * inv[None, :]
    cos, sin = jnp.cos(ang), jnp.sin(ang)
    return jnp.tile(cos, (1, 4)), jnp.concatenate([-sin, -sin, sin, sin], axis=1)


def _trunk(x, w, tabs, cos_t, sin_t):
    batch, seq_len, _ = x.shape
    xt = x.reshape(batch * seq_len, D_MODEL)
    m_tab, xif, xib, zft, zbt, dcf, dcb = tabs
    for layer in range(DEPTH):
        proj, kt, sf = _inproj(xt, layer, w, (zft, dcf), cos_t, sin_t, seq_len)
        z = _mixer(proj, kt, sf, layer, w, (xif, xib, zbt, dcb, m_tab), seq_len)
        xt = _mlp(xt, z, layer, w)
    return xt.reshape(batch, seq_len, D_MODEL)


def kernel(x_prompt, x_sample, norm_mix_pre, norm_mix_post, w_in, ret_decay_fwd, ret_decay_bwd, ret_gn,
           pool_w, pool_scale, w_out, norm_mlp_pre, norm_mlp_post, w_mlp1, w_mlp2):
    order = _qk_column_order()
    qk_cols = np.concatenate([order, D_QK + order])
    w = dict(
        w_qk=w_in[:, :, :2 * D_QK][:, :, qk_cols].astype(BF16), w_in=w_in.astype(BF16),
        w_u=_fold_pool(w_in, pool_w, pool_scale), w_out=w_out.astype(BF16),
        w_mlp1=w_mlp1.astype(BF16), w_mlp2=w_mlp2.astype(BF16),
        g_mix_pre=norm_mix_pre[:, None, :], g_mix_post=norm_mix_post[:, None, :], gn=ret_gn[:, None, :],
        g_mlp_pre=norm_mlp_pre[:, None, :], g_mlp_post=norm_mlp_post[:, None, :])
    dec = jnp.concatenate([ret_decay_fwd, ret_decay_bwd], axis=1).astype(F32)
    tabs = _decay_tables(dec)
    cos_t, sin_t = _rope_tables(max(x_prompt.shape[1], x_sample.shape[1]))
    y_prompt = _trunk(x_prompt, w, tabs, cos_t, sin_t)
    y_sample = _trunk(x_sample, w, tabs, cos_t, sin_t)
    return (y_prompt, y_sample)
```

```python
import functools

import numpy as np
import jax
import jax.numpy as jnp
from jax import lax
from jax.experimental import pallas as pl
from jax.experimental.pallas import tpu as pltpu

D_MODEL = 1024
DEPTH = 4
HEADS = 8
DK = 64
DV = 128
PAIRS = HEADS // 2
D_QK = HEADS * DK
CHUNK = 128
ROPE_BASE = 10000.0
POOL_WINDOWS = (2, 4, 8, 16)
POOL_GW = 256
D_FF = 4 * D_MODEL
EPS = 1e-6
QK_SCALE = DK ** -0.5

OFF_U, OFF_V, OFF_A, OFF_SGP, OFF_Q = 0, 1024, 2048, 3072, 4096
D_PROJ = 4608
W_V, W_G, W_U, W_GR, W_GP = 1024, 2048, 3072, 4096, 5120
HALO = 16
POOL_SUB = 128
POOL_WIN = 256
BAND_INTERIOR, BAND_FIRST, BAND_LAST = 0, 1, 2

TB_IN = 512
TB_MIX = 512
WOUT_ROWS, WOUT_COLS = 256, 256
TB_MLP = 512
VMEM_LIMIT_BYTES = 56 * 1024 * 1024

F32 = jnp.float32
BF16 = jnp.bfloat16


def _rms(x, g):
    return x * lax.rsqrt(jnp.mean(x * x, axis=-1, keepdims=True) + EPS) * g


def _zero_like_row(row):
    bits = lax.bitcast_convert_type(row, jnp.uint32)
    return lax.bitcast_convert_type((bits >> 16) >> 16, F32)


def _log_sigmoid(d):
    return jnp.minimum(d, 0.0) - jnp.log1p(jnp.exp(-jnp.abs(d)))


def _params():
    return pltpu.CompilerParams(dimension_semantics=("arbitrary",), vmem_limit_bytes=VMEM_LIMIT_BYTES)


def _tables_kernel(dec_ref, m_ref, xif_ref, xib_ref, zft_ref, zbt_ref, dcf_ref, dcb_ref):
    l = pl.program_id(0)
    c = CHUNK
    df = [dec_ref[l, h] for h in range(HEADS)]
    db = [dec_ref[l, HEADS + h] for h in range(HEADS)]

    ii = lax.broadcasted_iota(jnp.int32, (c, c), 0)
    jj = lax.broadcasted_iota(jnp.int32, (c, c), 1)
    dist = jnp.abs(ii - jj).astype(F32)
    for h in range(HEADS):
        dsel = jnp.where(ii >= jj, jnp.full((c, c), df[h], F32), jnp.full((c, c), db[h], F32))
        m_ref[0, h // 2, :, (h % 2) * c:(h % 2 + 1) * c] = jnp.exp(dist * _log_sigmoid(dsel))

    rows = xif_ref.shape[1]
    lane = lax.broadcasted_iota(jnp.int32, (rows, D_QK), 1)
    head = 2 * (lane >> 7) + ((lane >> 5) & 1)
    idx = (lax.broadcasted_iota(jnp.int32, (rows, D_QK), 0) & (c - 1)).astype(F32)
    sel_f = jnp.full((rows, D_QK), df[0], F32)
    sel_b = jnp.full((rows, D_QK), db[0], F32)
    for h in range(1, HEADS):
        sel_f = jnp.where(head == h, df[h], sel_f)
        sel_b = jnp.where(head == h, db[h], sel_b)
    xif_ref[0] = jnp.exp((idx + 1.0) * _log_sigmoid(sel_f))
    xib_ref[0] = jnp.exp((c - idx) * _log_sigmoid(sel_b))

    rl = lax.broadcasted_iota(jnp.int32, (2 * DK, c), 0)
    odd = ((rl >> 5) & 1) == 1
    jc = lax.broadcasted_iota(jnp.int32, (2 * DK, c), 1).astype(F32)
    for p in range(PAIRS):
        lg_f = _log_sigmoid(jnp.where(odd, df[2 * p + 1], df[2 * p]))
        lg_b = _log_sigmoid(jnp.where(odd, db[2 * p + 1], db[2 * p]))
        zft_ref[0, p] = jnp.exp((c - 1.0 - jc) * lg_f)
        zbt_ref[0, p] = jnp.exp(jc * lg_b)
        dcf_ref[0, p] = jnp.exp(float(c) * lg_f)
        dcb_ref[0, p] = jnp.exp(float(c) * lg_b)


def _decay_tables(dec):
    c = CHUNK
    pair_tab = jax.ShapeDtypeStruct((DEPTH, PAIRS, 2 * DK, c), F32)
    pair_spec = pl.BlockSpec((1, PAIRS, 2 * DK, c), lambda l: (l, 0, 0, 0))
    lane_tab = jax.ShapeDtypeStruct((DEPTH, c, D_QK), F32)
    lane_spec = pl.BlockSpec((1, c, D_QK), lambda l: (l, 0, 0))
    return pl.pallas_call(
        _tables_kernel,
        grid=(DEPTH,),
        in_specs=[pl.BlockSpec(memory_space=pltpu.SMEM)],
        out_specs=(pl.BlockSpec((1, PAIRS, c, 2 * c), lambda l: (l, 0, 0, 0)),
                   lane_spec, lane_spec, pair_spec, pair_spec, pair_spec, pair_spec),
        out_shape=(jax.ShapeDtypeStruct((DEPTH, PAIRS, c, 2 * c), F32),
                   lane_tab, lane_tab, pair_tab, pair_tab, pair_tab, pair_tab),
        compiler_params=_params(),
    )(dec)


def _fold_pool_kernel(wu_ref, pw_ref, ps_ref, out_ref):
    prod = jnp.dot(wu_ref[...].astype(BF16), pw_ref[...].astype(BF16), preferred_element_type=F32)
    out_ref[...] = (prod * ps_ref[...]).astype(out_ref.dtype)


def _fold_pool(w_in, pool_w, pool_scale):
    n_groups = len(POOL_WINDOWS)
    u_blk = W_U // POOL_GW
    return pl.pallas_call(
        _fold_pool_kernel,
        grid=(DEPTH, n_groups),
        in_specs=[
            pl.BlockSpec((None, D_MODEL, POOL_GW), lambda l, g: (l, 0, u_blk + g)),
            pl.BlockSpec((None, None, POOL_GW, POOL_GW), lambda l, g: (l, g, 0, 0)),
            pl.BlockSpec((None, 1, POOL_GW), lambda l, g: (l, 0, g)),
        ],
        out_specs=pl.BlockSpec((None, D_MODEL, POOL_GW), lambda l, g: (l, 0, g)),
        out_shape=jax.ShapeDtypeStruct((DEPTH, D_MODEL, D_MODEL), BF16),
        compiler_params=pltpu.CompilerParams(dimension_semantics=("arbitrary", "arbitrary"),
                                             vmem_limit_bytes=VMEM_LIMIT_BYTES),
    )(w_in, pool_w, pool_scale[:, None, :])


def _inproj_kernel(seq_len, x_ref, g_ref, wqk_ref, w_ref, wu_ref, cos_ref, sin_ref,
                   zft_ref, dcf_ref, gn_ref, proj_ref, kt_ref, sf_ref, k_scr, stf):
    tb = x_ref.shape[0]

    @pl.when(lax.rem(pl.program_id(0) * tb, seq_len) == 0)
    def _():
        stf[...] = jnp.zeros_like(stf)

    half = tb // 2
    h_top = _rms(x_ref[0:half, :], g_ref[...]).astype(BF16)
    qk_top = jnp.dot(h_top, wqk_ref[...], preferred_element_type=F32)
    h_bot = _rms(x_ref[half:tb, :], g_ref[...]).astype(BF16)
    qk_bot = jnp.dot(h_bot, wqk_ref[...], preferred_element_type=F32)
    h = jnp.concatenate([h_top, h_bot], axis=0)
    qk = jnp.concatenate([qk_top, qk_bot], axis=0)
    cos = cos_ref[...]
    sin = sin_ref[...]

    for p in range(PAIRS):
        lo = p * 128
        qp = qk[:, lo:lo + 128]
        qr = (qp * cos + pltpu.roll(qp, 64, 1) * sin) * QK_SCALE
        proj_ref[:, OFF_Q + lo:OFF_Q + lo + 128] = qr.astype(BF16)
        kp = qk[:, D_QK + lo:D_QK + lo + 128]
        kr = kp * cos + pltpu.roll(kp, 64, 1) * sin
        k_scr[:, lo:lo + 128] = kr

    def section(w_off):
        return jnp.dot(h, w_ref[:, w_off:w_off + D_MODEL], preferred_element_type=F32)

    v = section(W_V).astype(BF16)
    proj_ref[:, OFF_V:OFF_V + D_MODEL] = v

    rl = lax.broadcasted_iota(jnp.int32, (2 * DK, DV), 0)
    even_row = ((rl >> 5) & 1) == 0

    def summaries(c):
        r0 = c * CHUNK
        for p in range(PAIRS):
            kt = k_scr[r0:r0 + CHUNK, p * 128:(p + 1) * 128].T
            kt_ref[c, p] = kt.astype(BF16)
            vp = v[r0:r0 + CHUNK, 2 * p * DV:(2 * p + 2) * DV]
            kv = jnp.dot((kt * zft_ref[p]).astype(BF16), vp, preferred_element_type=F32)
            cur = stf[p]
            sf_ref[c, p] = cur.astype(BF16)
            stf[p] = cur * dcf_ref[p] + jnp.where(even_row, kv[:, 0:DV], kv[:, DV:2 * DV])

    g = section(W_G)
    a_gate = g * jax.nn.sigmoid(g) * jax.nn.sigmoid(section(W_GR)) * gn_ref[...]
    proj_ref[:, OFF_A:OFF_A + D_MODEL] = a_gate.astype(BF16)
    proj_ref[:, OFF_U:OFF_U + D_MODEL] = jnp.dot(h, wu_ref[...], preferred_element_type=F32).astype(BF16)
    proj_ref[:, OFF_SGP:OFF_SGP + D_MODEL] = jax.nn.sigmoid(section(W_GP)).astype(BF16)
    for c in range(tb // CHUNK):
        summaries(c)


def _inproj(x, layer, w, tabs, cos_t, sin_t, seq_len):
    t = x.shape[0]
    tb = TB_IN
    nb = t // tb
    nb_seq = seq_len // tb
    cpb = tb // CHUNK
    zft, dcf = tabs
    row_l = pl.BlockSpec((None, 1, D_MODEL), lambda i: (layer, 0, 0))
    pair_l = pl.BlockSpec((None, PAIRS, 2 * DK, CHUNK), lambda i: (layer, 0, 0, 0))
    rope_blk = pl.BlockSpec((tb, 128), lambda i: (i % nb_seq, 0))
    state_blk = pl.BlockSpec((cpb, PAIRS, 2 * DK, DV), lambda i: (i, 0, 0, 0))
    return pl.pallas_call(
        functools.partial(_inproj_kernel, seq_len),
        grid=(nb,),
        in_specs=[
            pl.BlockSpec((tb, D_MODEL), lambda i: (i, 0)),
            row_l,
            pl.BlockSpec((None,) + w["w_qk"].shape[1:], lambda i: (layer, 0, 0), pipeline_mode=pl.Buffered(1)),
            pl.BlockSpec((None,) + w["w_in"].shape[1:], lambda i: (layer, 0, 0), pipeline_mode=pl.Buffered(1)),
            pl.BlockSpec((None, D_MODEL, D_MODEL), lambda i: (layer, 0, 0), pipeline_mode=pl.Buffered(1)),
            rope_blk, rope_blk,
            pair_l, pair_l,
            row_l,
        ],
        out_specs=(pl.BlockSpec((tb, D_PROJ), lambda i: (i, 0)), state_blk, state_blk),
        out_shape=(
            jax.ShapeDtypeStruct((t, D_PROJ), BF16),
            jax.ShapeDtypeStruct((t // CHUNK, PAIRS, 2 * DK, CHUNK), BF16),
            jax.ShapeDtypeStruct((t // CHUNK, PAIRS, 2 * DK, DV), BF16),
        ),
        scratch_shapes=[pltpu.VMEM((tb, D_QK), F32), pltpu.VMEM((PAIRS, 2 * DK, DV), F32)],
        compiler_params=_params(),
    )(x, w["g_mix_pre"], w["w_qk"], w["w_in"], w["w_u"], cos_t, sin_t, zft, dcf, w["gn"])


def _mixer_kernel(seq_len, proj_ref, uprev_ref, unext_ref, kt_ref, sf_ref, xif_ref, xib_ref, zbt_ref, dcb_ref,
                  m_ref, band_ref, wout_ref, out_ref, yr_scr, yp_scr, u_scr, m_scr, stb):
    tb = proj_ref.shape[0]
    c = CHUNK
    step = pl.program_id(0)
    nblk = pl.num_programs(0) - 1
    blk = nblk - 1 - jnp.minimum(step, nblk - 1)
    s0 = lax.rem(blk * tb, seq_len)
    slot = lax.rem(step, 2)

    @pl.when(step == 0)
    def _():
        m_scr[1] = jnp.zeros(m_scr.shape[1:], BF16)

    @pl.when(s0 + tb == seq_len)
    def _():
        stb[...] = jnp.zeros_like(stb)

    def wout_piece(idx):
        rh, nt = divmod(idx, D_MODEL // WOUT_COLS)
        rows = slice(rh * WOUT_ROWS, (rh + 1) * WOUT_ROWS)
        cols = slice(nt * WOUT_COLS, (nt + 1) * WOUT_COLS)
        out_ref[rows, cols] = jnp.dot(m_scr[1 - slot, rows, :], wout_ref[:, cols], preferred_element_type=F32)

    keep_prev = jnp.where(s0 > 0, 1.0, 0.0)
    keep_next = jnp.where(s0 + tb < seq_len, 1.0, 0.0)
    u_scr[0:HALO, :] = (uprev_ref[...].astype(F32) * keep_prev).astype(BF16)
    u_scr[HALO:HALO + tb, :] = proj_ref[:, OFF_U:OFF_U + D_MODEL]
    u_scr[HALO + tb:2 * HALO + tb, :] = (unext_ref[...].astype(F32) * keep_next).astype(BF16)
    u_scr[2 * HALO + tb:, :] = jnp.zeros((u_scr.shape[0] - 2 * HALO - tb, D_MODEL), BF16)
    nsub = tb // POOL_SUB
    first_var = jnp.where(s0 == 0, BAND_FIRST, BAND_INTERIOR)
    last_var = jnp.where(s0 + tb == seq_len, BAND_LAST, BAND_INTERIOR)

    def pool_group(gi):
        lo = gi * POOL_GW
        for sb in range(nsub):
            r0 = sb * POOL_SUB
            var_idx = first_var if sb == 0 else (last_var if sb == nsub - 1 else BAND_INTERIOR)
            yp_scr[r0:r0 + POOL_SUB, lo:lo + POOL_GW] = jnp.dot(
                band_ref[var_idx, gi], u_scr[r0:r0 + POOL_WIN, lo:lo + POOL_GW], preferred_element_type=F32)

    row = lax.broadcasted_iota(jnp.int32, (2 * DK, DV), 0)
    even_row = ((row >> 5) & 1) == 0
    row_even = even_row.astype(F32)
    s_mask = (row_even.astype(BF16), (1.0 - row_even).astype(BF16))
    zeros_v = jnp.zeros((c, DV), BF16)
    order = [(ci, p) for ci in reversed(range(tb // c)) for p in range(PAIRS)]

    def expand(w):
        return jnp.concatenate([w * s_mask[0], w * s_mask[1]], axis=1)

    def early(j):
        ci, p = order[j]
        rows = slice(ci * c, (ci + 1) * c)
        kt = kt_ref[ci, p]
        q = proj_ref[rows, OFF_Q + p * 128:OFF_Q + (p + 1) * 128]
        s = jnp.dot(q, expand(kt), preferred_element_type=F32)
        pm = (s * m_ref[p]).astype(BF16)
        vp = proj_ref[rows, OFF_V + 2 * p * DV:OFF_V + (2 * p + 2) * DV]
        kv = jnp.dot((kt.astype(F32) * zbt_ref[p]).astype(BF16), vp, preferred_element_type=F32)
        qf32 = q.astype(F32)
        lanes = slice(p * 128, (p + 1) * 128)
        q_cross = jnp.concatenate([(qf32 * xif_ref[:, lanes]).astype(BF16),
                                   (qf32 * xib_ref[:, lanes]).astype(BF16)], axis=1)
        return pm, jnp.where(even_row, kv[:, 0:DV], kv[:, DV:2 * DV]), q_cross

    def values(j, early_out):
        pm, kvb, q_cross = early_out
        ci, p = order[j]
        rows = slice(ci * c, (ci + 1) * c)
        lo = p * 128
        v0 = proj_ref[rows, OFF_V + 2 * lo:OFF_V + 2 * lo + DV]
        v1 = proj_ref[rows, OFF_V + 2 * lo + DV:OFF_V + 2 * lo + 2 * DV]
        cur_b = stb[p]
        stb[p] = cur_b * dcb_ref[p] + kvb
        rhs = jnp.concatenate(
            [jnp.concatenate([v0, zeros_v], axis=1),
             jnp.concatenate([zeros_v, v1], axis=1),
             expand(sf_ref[ci, p]),
             expand(cur_b.astype(BF16))], axis=0)
        o2 = jnp.dot(jnp.concatenate([pm, q_cross], axis=1), rhs, preferred_element_type=F32)
        for hh in range(2):
            hl = (2 * p + hh) * DV
            o = o2[:, hh * DV:(hh + 1) * DV]
            mu = jnp.mean(o, axis=-1, keepdims=True)
            d = o - mu
            var = jnp.mean(d * d, axis=-1, keepdims=True)
            a = proj_ref[rows, OFF_A + hl:OFF_A + hl + DV].astype(F32)
            yr_scr[rows, hl:hl + DV] = a * (d * lax.rsqrt(var + EPS))
        if p == PAIRS - 1:
            sgp = proj_ref[rows, OFF_SGP:OFF_SGP + D_MODEL].astype(F32)
            m_scr[slot, rows, :] = (yr_scr[rows, :] + sgp * yp_scr[rows, :]).astype(BF16)

    n_wout = (tb // WOUT_ROWS) * (D_MODEL // WOUT_COLS)
    fillers = {0: lambda: pool_group(0), 1: lambda: pool_group(1), 2: lambda: pool_group(2),
               3: lambda: pool_group(3)}
    for k in range(1, n_wout - 1):
        fillers[2 + 2 * k] = functools.partial(wout_piece, k)
    assert max(fillers) < len(order)

    wout_piece(0)
    pending = {0: early(0), 1: early(1)}
    for j in range(len(order)):
        if j in fillers:
            fillers[j]()
        values(j, pending.pop(j))
        if j + 2 < len(order):
            pending[j + 2] = early(j + 2)
    wout_piece(n_wout - 1)


def _pool_bands():
    i = np.arange(POOL_SUB)[:, None]
    j = np.arange(POOL_WIN)[None, :] - HALO
    out = np.zeros((3, len(POOL_WINDOWS), POOL_SUB, POOL_WIN), np.float64)
    for gi, w in enumerate(POOL_WINDOWS):
        half = w // 2
        inside = ((j - i >= -half) & (j - i <= half - 1)).astype(np.float64)
        token = (j == i).astype(np.float64)
        counts = {
            BAND_INTERIOR: np.full((POOL_SUB, 1), float(w)),
            BAND_FIRST: (i + half) - np.maximum(i - half, 0),
            BAND_LAST: np.minimum(i + half, POOL_SUB) - (i - half),
        }
        for var, cnt in counts.items():
            out[var, gi] = inside / cnt - token
    return out.astype(np.float32)


def _mixer(proj, kt, sf, layer, w, tabs, seq_len):
    t = proj.shape[0]
    tb = TB_MIX
    nb = t // tb
    cpb = tb // CHUNK
    hb = tb // HALO
    n_halo = t // HALO
    u_col = OFF_U // D_MODEL
    xif, xib, zbt, dcb, m_tab = tabs
    band = jnp.asarray(_pool_bands(), BF16)
    rev = lambda i: nb - 1 - jnp.minimum(i, nb - 1)
    rev_out = lambda i: nb - 1 - jnp.maximum(i - 1, 0)
    lay3 = lambda i: (layer, 0, 0)
    lay4 = lambda i: (layer, 0, 0, 0)
    state_blk = pl.BlockSpec((cpb, PAIRS, 2 * DK, DV), lambda i: (rev(i), 0, 0, 0))
    return pl.pallas_call(
        functools.partial(_mixer_kernel, seq_len),
        grid=(nb + 1,),
        in_specs=[
            pl.BlockSpec((tb, D_PROJ), lambda i: (rev(i), 0)),
            pl.BlockSpec((HALO, D_MODEL), lambda i: (jnp.maximum(rev(i) * hb - 1, 0), u_col)),
            pl.BlockSpec((HALO, D_MODEL), lambda i: (jnp.minimum((rev(i) + 1) * hb, n_halo - 1), u_col)),
            state_blk, state_blk,
            pl.BlockSpec((None, CHUNK, D_QK), lay3),
            pl.BlockSpec((None, CHUNK, D_QK), lay3),
            pl.BlockSpec((None, PAIRS, 2 * DK, CHUNK), lay4),
            pl.BlockSpec((None, PAIRS, 2 * DK, DV), lay4),
            pl.BlockSpec((None, PAIRS, CHUNK, 2 * CHUNK), lay4),
            pl.BlockSpec(band.shape, lambda i: (0, 0, 0, 0)),
            pl.BlockSpec((None, D_MODEL, D_MODEL), lay3),
        ],
        out_specs=pl.BlockSpec((tb, D_MODEL), lambda i: (rev_out(i), 0)),
        out_shape=jax.ShapeDtypeStruct((t, D_MODEL), F32),
        scratch_shapes=[
            pltpu.VMEM((tb, D_MODEL), F32),
            pltpu.VMEM((tb, D_MODEL), F32),
            pltpu.VMEM((tb - POOL_SUB + POOL_WIN, D_MODEL), BF16),
            pltpu.VMEM((2, tb, D_MODEL), BF16),
            pltpu.VMEM((PAIRS, 2 * DK, DV), F32),
        ],
        compiler_params=_params(),
    )(proj, proj, proj, kt, sf, xif, xib, zbt, dcb, m_tab, band, w["w_out"])


def _mlp_body(x_ref, z_ref, gmix_ref, gpre_ref, w1_ref, w2_ref, gpost_ref, out_ref, x1_rd, h_rd, x1_wr, h_wr):
    tb = x_ref.shape[0]
    n_ff = D_FF // D_MODEL
    rows_per = tb // (2 * n_ff)

    def norms(k):
        rows = slice(k * rows_per, (k + 1) * rows_per)
        x1 = x_ref[rows, :] + _rms(z_ref[rows, :], gmix_ref[...])
        x1_wr[rows, :] = x1
        h_wr[rows, :] = _rms(x1, gpre_ref[...]).astype(BF16)
        return _zero_like_row(x1[0:1, :])

    h = h_rd[...]
    acc = jnp.zeros((tb, D_MODEL), F32)
    for j in range(n_ff):
        lo = j * D_MODEL
        floor = jnp.maximum(norms(2 * j), norms(2 * j + 1))
        f = jnp.dot(h, w1_ref[:, lo:lo + D_MODEL], preferred_element_type=F32)
        f = jnp.square(jnp.maximum(f, floor)).astype(BF16)
        acc = acc + jnp.dot(f, w2_ref[lo:lo + D_MODEL, :], preferred_element_type=F32)
    out_ref[...] = x1_rd[...] + _rms(acc, gpost_ref[...])


def _mlp_kernel(x_ref, z_ref, gmix_ref, gpre_ref, w1_ref, w2_ref, gpost_ref, out_ref, x1_a, h_a, x1_b, h_b):
    step = pl.program_id(0)
    args = (x_ref, z_ref, gmix_ref, gpre_ref, w1_ref, w2_ref, gpost_ref, out_ref)

    @pl.when(step == 0)
    def _():
        x1_b[...] = jnp.zeros_like(x1_b)
        h_b[...] = jnp.zeros_like(h_b)

    @pl.when(lax.rem(step, 2) == 0)
    def _():
        _mlp_body(*args, x1_b, h_b, x1_a, h_a)

    @pl.when(lax.rem(step, 2) == 1)
    def _():
        _mlp_body(*args, x1_a, h_a, x1_b, h_b)


def _mlp(x, z, layer, w):
    t = x.shape[0]
    tb = TB_MLP
    nb = t // tb
    lay3 = lambda i: (layer, 0, 0)
    row_l = pl.BlockSpec((None, 1, D_MODEL), lay3)
    cur = lambda i: (jnp.minimum(i, nb - 1), 0)
    return pl.pallas_call(
        _mlp_kernel,
        grid=(nb + 1,),
        in_specs=[
            pl.BlockSpec((tb, D_MODEL), cur),
            pl.BlockSpec((tb, D_MODEL), cur),
            row_l, row_l,
            pl.BlockSpec((None, D_MODEL, D_FF), lay3, pipeline_mode=pl.Buffered(1)),
            pl.BlockSpec((None, D_FF, D_MODEL), lay3, pipeline_mode=pl.Buffered(1)),
            row_l,
        ],
        out_specs=pl.BlockSpec((tb, D_MODEL), lambda i: (jnp.maximum(i - 1, 0), 0)),
        out_shape=jax.ShapeDtypeStruct((t, D_MODEL), F32),
        scratch_shapes=[pltpu.VMEM((tb, D_MODEL), F32), pltpu.VMEM((tb, D_MODEL), BF16),
                        pltpu.VMEM((tb, D_MODEL), F32), pltpu.VMEM((tb, D_MODEL), BF16)],
        compiler_params=_params(),
    )(x, z, w["g_mix_post"], w["g_mlp_pre"], w["w_mlp1"], w["w_mlp2"], w["g_mlp_post"])


def _qk_column_order():
    n = np.arange(D_QK)
    pair, l = n // 128, n % 128
    head = 2 * pair + (l // 32) % 2
    return head * DK + (l // 64) * (DK // 2) + l % 32


def _rope_tables(seq_len):
    half = DK // 2
    inv = ROPE_BASE ** (-jnp.arange(half, dtype=F32) / half)
    ang_i = (jnp.arange(seq_len // CHUNK, dtype=F32) * CHUNK)[:, None] * inv[None, :]
    ang_j = jnp.arange(CHUNK, dtype=F32)[:, None] * inv[None, :]
    ci, si = jnp.cos(ang_i)[:, None, :], jnp.sin(ang_i)[:, None, :]
    cj, sj = jnp.cos(ang_j)[None, :, :], jnp.sin(ang_j)[None, :, :]
    cos = (ci * cj - si * sj).reshape(seq_len, half)
    sin = (si * cj + ci * sj).reshape(seq_len, half)
    return jnp.tile(cos, (1, 4)), jnp.concatenate([-sin, -sin, sin, sin], axis=1)


def _trunk(x, w, tabs, cos_t, sin_t):
    batch, seq_len, _ = x.shape
    xt = x.reshape(batch * seq_len, D_MODEL)
    m_tab, xif, xib, zft, zbt, dcf, dcb = tabs
    for layer in range(DEPTH):
        proj, kt, sf = _inproj(xt, layer, w, (zft, dcf), cos_t, sin_t, seq_len)
        z = _mixer(proj, kt, sf, layer, w, (xif, xib, zbt, dcb, m_tab), seq_len)
        xt = _mlp(xt, z, layer, w)
    return xt.reshape(batch, seq_len, D_MODEL)


def kernel(x_prompt, x_sample, norm_mix_pre, norm_mix_post, w_in, ret_decay_fwd, ret_decay_bwd, ret_gn,
           pool_w, pool_scale, w_out, norm_mlp_pre, norm_mlp_post, w_mlp1, w_mlp2):
    order = _qk_column_order()
    qk_cols = np.concatenate([order, D_QK + order])
    w = dict(
        w_qk=w_in[:, :, :2 * D_QK].astype(BF16)[:, :, qk_cols], w_in=w_in.astype(BF16),
        w_u=_fold_pool(w_in, pool_w, pool_scale), w_out=w_out.astype(BF16),
        w_mlp1=w_mlp1.astype(BF16), w_mlp2=w_mlp2.astype(BF16),
        g_mix_pre=norm_mix_pre[:, None, :], g_mix_post=norm_mix_post[:, None, :], gn=ret_gn[:, None, :],
        g_mlp_pre=norm_mlp_pre[:, None, :], g_mlp_post=norm_mlp_post[:, None, :])
    dec = jnp.concatenate([ret_decay_fwd, ret_decay_bwd], axis=1).astype(F32)
    tabs = _decay_tables(dec)
    cos_t, sin_t = _rope_tables(max(x_prompt.shape[1], x_sample.shape[1]))
    y_prompt = _trunk(x_prompt, w, tabs, cos_t, sin_t)
    y_sample = _trunk(x_sample, w, tabs, cos_t, sin_t)
    return (y_prompt, y_sample)
```

```python
import functools

import numpy as np
import jax
import jax.numpy as jnp
from jax import lax
from jax.experimental import pallas as pl
from jax.experimental.pallas import tpu as pltpu

D_MODEL = 1024
DEPTH = 4
HEADS = 8
DK = 64
DV = 128
PAIRS = HEADS // 2
D_QK = HEADS * DK
CHUNK = 128
ROPE_BASE = 10000.0
POOL_WINDOWS = (2, 4, 8, 16)
POOL_GW = 256
D_FF = 4 * D_MODEL
EPS = 1e-6
QK_SCALE = DK ** -0.5

OFF_U, OFF_V, OFF_A, OFF_SGP, OFF_Q = 0, 1024, 2048, 3072, 4096
D_PROJ = 4608
W_V, W_G, W_U, W_GR, W_GP = 1024, 2048, 3072, 4096, 5120
HALO = 16
POOL_SUB = 128
POOL_WIN = 256
BAND_INTERIOR, BAND_FIRST, BAND_LAST = 0, 1, 2

TB_IN = 512
TB_MIX = 512
WOUT_ROWS, WOUT_COLS = 256, 256
TB_MLP = 512
FF_CHUNK = 512
VMEM_LIMIT_BYTES = 56 * 1024 * 1024

F32 = jnp.float32
BF16 = jnp.bfloat16


def _rms(x, g):
    return x * lax.rsqrt(jnp.mean(x * x, axis=-1, keepdims=True) + EPS) * g


def _zero_like_row(row):
    bits = lax.bitcast_convert_type(row, jnp.uint32)
    return lax.bitcast_convert_type((bits >> 16) >> 16, F32)


def _log_sigmoid(d):
    return jnp.minimum(d, 0.0) - jnp.log1p(jnp.exp(-jnp.abs(d)))


def _params():
    return pltpu.CompilerParams(dimension_semantics=("arbitrary",), vmem_limit_bytes=VMEM_LIMIT_BYTES)


def _tables_kernel(dec_ref, m_ref, xif_ref, xib_ref, zft_ref, zbt_ref, dcf_ref, dcb_ref):
    l = pl.program_id(0)
    c = CHUNK
    df = [dec_ref[l, h] for h in range(HEADS)]
    db = [dec_ref[l, HEADS + h] for h in range(HEADS)]

    ii = lax.broadcasted_iota(jnp.int32, (c, c), 0)
    jj = lax.broadcasted_iota(jnp.int32, (c, c), 1)
    dist = jnp.abs(ii - jj).astype(F32)
    for h in range(HEADS):
        dsel = jnp.where(ii >= jj, jnp.full((c, c), df[h], F32), jnp.full((c, c), db[h], F32))
        m_ref[0, h // 2, :, (h % 2) * c:(h % 2 + 1) * c] = jnp.exp(dist * _log_sigmoid(dsel))

    rows = xif_ref.shape[1]
    lane = lax.broadcasted_iota(jnp.int32, (rows, D_QK), 1)
    head = 2 * (lane >> 7) + ((lane >> 5) & 1)
    idx = (lax.broadcasted_iota(jnp.int32, (rows, D_QK), 0) & (c - 1)).astype(F32)
    sel_f = jnp.full((rows, D_QK), df[0], F32)
    sel_b = jnp.full((rows, D_QK), db[0], F32)
    for h in range(1, HEADS):
        sel_f = jnp.where(head == h, df[h], sel_f)
        sel_b = jnp.where(head == h, db[h], sel_b)
    xif_ref[0] = jnp.exp((idx + 1.0) * _log_sigmoid(sel_f))
    xib_ref[0] = jnp.exp((c - idx) * _log_sigmoid(sel_b))

    rl = lax.broadcasted_iota(jnp.int32, (2 * DK, c), 0)
    odd = ((rl >> 5) & 1) == 1
    jc = lax.broadcasted_iota(jnp.int32, (2 * DK, c), 1).astype(F32)
    for p in range(PAIRS):
        lg_f = _log_sigmoid(jnp.where(odd, df[2 * p + 1], df[2 * p]))
        lg_b = _log_sigmoid(jnp.where(odd, db[2 * p + 1], db[2 * p]))
        zft_ref[0, p] = jnp.exp((c - 1.0 - jc) * lg_f)
        zbt_ref[0, p] = jnp.exp(jc * lg_b)
        dcf_ref[0, p] = jnp.exp(float(c) * lg_f)
        dcb_ref[0, p] = jnp.exp(float(c) * lg_b)


def _decay_tables(dec):
    c = CHUNK
    pair_tab = jax.ShapeDtypeStruct((DEPTH, PAIRS, 2 * DK, c), F32)
    pair_spec = pl.BlockSpec((1, PAIRS, 2 * DK, c), lambda l: (l, 0, 0, 0))
    lane_tab = jax.ShapeDtypeStruct((DEPTH, c, D_QK), F32)
    lane_spec = pl.BlockSpec((1, c, D_QK), lambda l: (l, 0, 0))
    return pl.pallas_call(
        _tables_kernel,
        grid=(DEPTH,),
        in_specs=[pl.BlockSpec(memory_space=pltpu.SMEM)],
        out_specs=(pl.BlockSpec((1, PAIRS, c, 2 * c), lambda l: (l, 0, 0, 0)),
                   lane_spec, lane_spec, pair_spec, pair_spec, pair_spec, pair_spec),
        out_shape=(jax.ShapeDtypeStruct((DEPTH, PAIRS, c, 2 * c), F32),
                   lane_tab, lane_tab, pair_tab, pair_tab, pair_tab, pair_tab),
        compiler_params=_params(),
    )(dec)


def _fold_pool_kernel(wu_ref, pw_ref, ps_ref, out_ref):
    for g in range(len(POOL_WINDOWS)):
        cols = slice(g * POOL_GW, (g + 1) * POOL_GW)
        prod = jnp.dot(wu_ref[:, cols].astype(BF16), pw_ref[g].astype(BF16), preferred_element_type=F32)
        out_ref[:, cols] = (prod * ps_ref[:, cols]).astype(out_ref.dtype)


def _fold_pool(w_in, pool_w, pool_scale):
    n_groups = len(POOL_WINDOWS)
    return pl.pallas_call(
        _fold_pool_kernel,
        grid=(DEPTH,),
        in_specs=[
            pl.BlockSpec((None, D_MODEL, D_MODEL), lambda l: (l, 0, W_U // D_MODEL)),
            pl.BlockSpec((None, n_groups, POOL_GW, POOL_GW), lambda l: (l, 0, 0, 0)),
            pl.BlockSpec((None, 1, D_MODEL), lambda l: (l, 0, 0)),
        ],
        out_specs=pl.BlockSpec((None, D_MODEL, D_MODEL), lambda l: (l, 0, 0)),
        out_shape=jax.ShapeDtypeStruct((DEPTH, D_MODEL, D_MODEL), BF16),
        compiler_params=_params(),
    )(w_in, pool_w, pool_scale[:, None, :])


def _inproj_kernel(seq_len, x_ref, g_ref, wqk_ref, w_ref, wu_ref, cos_ref, sin_ref,
                   zft_ref, dcf_ref, gn_ref, proj_ref, kt_ref, sf_ref, k_scr, stf):
    tb = x_ref.shape[0]

    @pl.when(lax.rem(pl.program_id(0) * tb, seq_len) == 0)
    def _():
        stf[...] = jnp.zeros_like(stf)

    half = tb // 2
    h_top = _rms(x_ref[0:half, :], g_ref[...]).astype(BF16)
    qk_top = jnp.dot(h_top, wqk_ref[...], preferred_element_type=F32)
    h_bot = _rms(x_ref[half:tb, :], g_ref[...]).astype(BF16)
    qk_bot = jnp.dot(h_bot, wqk_ref[...], preferred_element_type=F32)
    h = jnp.concatenate([h_top, h_bot], axis=0)
    qk = jnp.concatenate([qk_top, qk_bot], axis=0)
    cos = cos_ref[...]
    sin = sin_ref[...]

    for p in range(PAIRS):
        lo = p * 128
        qp = qk[:, lo:lo + 128]
        qr = (qp * cos + pltpu.roll(qp, 64, 1) * sin) * QK_SCALE
        proj_ref[:, OFF_Q + lo:OFF_Q + lo + 128] = qr.astype(BF16)
        kp = qk[:, D_QK + lo:D_QK + lo + 128]
        kr = kp * cos + pltpu.roll(kp, 64, 1) * sin
        k_scr[:, lo:lo + 128] = kr

    def section(w_off):
        return jnp.dot(h, w_ref[:, w_off:w_off + D_MODEL], preferred_element_type=F32)

    v = section(W_V).astype(BF16)
    proj_ref[:, OFF_V:OFF_V + D_MODEL] = v

    rl = lax.broadcasted_iota(jnp.int32, (2 * DK, DV), 0)
    even_row = ((rl >> 5) & 1) == 0

    def summaries(c):
        r0 = c * CHUNK
        for p in range(PAIRS):
            kt = k_scr[r0:r0 + CHUNK, p * 128:(p + 1) * 128].T
            kt_ref[c, p] = kt.astype(BF16)
            vp = v[r0:r0 + CHUNK, 2 * p * DV:(2 * p + 2) * DV]
            kv = jnp.dot((kt * zft_ref[p]).astype(BF16), vp, preferred_element_type=F32)
            cur = stf[p]
            sf_ref[c, p] = cur.astype(BF16)
            stf[p] = cur * dcf_ref[p] + jnp.where(even_row, kv[:, 0:DV], kv[:, DV:2 * DV])

    g = section(W_G)
    a_gate = g * jax.nn.sigmoid(g) * jax.nn.sigmoid(section(W_GR)) * gn_ref[...]
    proj_ref[:, OFF_A:OFF_A + D_MODEL] = a_gate.astype(BF16)
    proj_ref[:, OFF_U:OFF_U + D_MODEL] = jnp.dot(h, wu_ref[...], preferred_element_type=F32).astype(BF16)
    proj_ref[:, OFF_SGP:OFF_SGP + D_MODEL] = jax.nn.sigmoid(section(W_GP)).astype(BF16)
    for c in range(tb // CHUNK):
        summaries(c)


def _inproj(x, layer, w, tabs, cos_t, sin_t, seq_len):
    t = x.shape[0]
    tb = TB_IN
    nb = t // tb
    nb_seq = seq_len // tb
    cpb = tb // CHUNK
    zft, dcf = tabs
    row_l = pl.BlockSpec((None, 1, D_MODEL), lambda i: (layer, 0, 0))
    pair_l = pl.BlockSpec((None, PAIRS, 2 * DK, CHUNK), lambda i: (layer, 0, 0, 0))
    rope_blk = pl.BlockSpec((tb, 128), lambda i: (i % nb_seq, 0))
    state_blk = pl.BlockSpec((cpb, PAIRS, 2 * DK, DV), lambda i: (i, 0, 0, 0))
    return pl.pallas_call(
        functools.partial(_inproj_kernel, seq_len),
        grid=(nb,),
        in_specs=[
            pl.BlockSpec((tb, D_MODEL), lambda i: (i, 0)),
            row_l,
            pl.BlockSpec((None,) + w["w_qk"].shape[1:], lambda i: (layer, 0, 0), pipeline_mode=pl.Buffered(1)),
            pl.BlockSpec((None,) + w["w_in"].shape[1:], lambda i: (layer, 0, 0), pipeline_mode=pl.Buffered(1)),
            pl.BlockSpec((None, D_MODEL, D_MODEL), lambda i: (layer, 0, 0), pipeline_mode=pl.Buffered(1)),
            rope_blk, rope_blk,
            pair_l, pair_l,
            row_l,
        ],
        out_specs=(pl.BlockSpec((tb, D_PROJ), lambda i: (i, 0)), state_blk, state_blk),
        out_shape=(
            jax.ShapeDtypeStruct((t, D_PROJ), BF16),
            jax.ShapeDtypeStruct((t // CHUNK, PAIRS, 2 * DK, CHUNK), BF16),
            jax.ShapeDtypeStruct((t // CHUNK, PAIRS, 2 * DK, DV), BF16),
        ),
        scratch_shapes=[pltpu.VMEM((tb, D_QK), F32), pltpu.VMEM((PAIRS, 2 * DK, DV), F32)],
        compiler_params=_params(),
    )(x, w["g_mix_pre"], w["w_qk"], w["w_in"], w["w_u"], cos_t, sin_t, zft, dcf, w["gn"])


def _mixer_kernel(seq_len, proj_ref, uprev_ref, unext_ref, kt_ref, sf_ref, xif_ref, xib_ref, zbt_ref, dcb_ref,
                  m_ref, band_ref, wout_ref, out_ref, yr_scr, yp_scr, u_scr, m_scr, stb):
    step = pl.program_id(0)
    nblk = pl.num_programs(0) - 1

    @pl.when(step < nblk)
    def _():
        _mixer_step(seq_len, proj_ref, uprev_ref, unext_ref, kt_ref, sf_ref, xif_ref, xib_ref, zbt_ref, dcb_ref,
                    m_ref, band_ref, wout_ref, out_ref, yr_scr, yp_scr, u_scr, m_scr, stb)

    @pl.when(step == nblk)
    def _():
        out_ref[...] = jnp.dot(m_scr[1 - lax.rem(step, 2)], wout_ref[...], preferred_element_type=F32)


def _mixer_step(seq_len, proj_ref, uprev_ref, unext_ref, kt_ref, sf_ref, xif_ref, xib_ref, zbt_ref, dcb_ref,
                m_ref, band_ref, wout_ref, out_ref, yr_scr, yp_scr, u_scr, m_scr, stb):
    tb = proj_ref.shape[0]
    c = CHUNK
    step = pl.program_id(0)
    nblk = pl.num_programs(0) - 1
    blk = nblk - 1 - step
    s0 = lax.rem(blk * tb, seq_len)
    slot = lax.rem(step, 2)

    @pl.when(step == 0)
    def _():
        m_scr[1] = jnp.zeros(m_scr.shape[1:], BF16)

    @pl.when(s0 + tb == seq_len)
    def _():
        stb[...] = jnp.zeros_like(stb)

    def wout_piece(idx):
        rh, nt = divmod(idx, D_MODEL // WOUT_COLS)
        rows = slice(rh * WOUT_ROWS, (rh + 1) * WOUT_ROWS)
        cols = slice(nt * WOUT_COLS, (nt + 1) * WOUT_COLS)
        out_ref[rows, cols] = jnp.dot(m_scr[1 - slot, rows, :], wout_ref[:, cols], preferred_element_type=F32)

    keep_prev = jnp.where(s0 > 0, 1.0, 0.0)
    keep_next = jnp.where(s0 + tb < seq_len, 1.0, 0.0)
    u_scr[0:HALO, :] = (uprev_ref[...].astype(F32) * keep_prev).astype(BF16)
    u_scr[HALO:HALO + tb, :] = proj_ref[:, OFF_U:OFF_U + D_MODEL]
    u_scr[HALO + tb:2 * HALO + tb, :] = (unext_ref[...].astype(F32) * keep_next).astype(BF16)
    u_scr[2 * HALO + tb:, :] = jnp.zeros((u_scr.shape[0] - 2 * HALO - tb, D_MODEL), BF16)
    nsub = tb // POOL_SUB
    first_var = jnp.where(s0 == 0, BAND_FIRST, BAND_INTERIOR)
    last_var = jnp.where(s0 + tb == seq_len, BAND_LAST, BAND_INTERIOR)

    def pool_group(gi):
        lo = gi * POOL_GW
        for sb in range(nsub):
            r0 = sb * POOL_SUB
            var_idx = first_var if sb == 0 else (last_var if sb == nsub - 1 else BAND_INTERIOR)
            yp_scr[r0:r0 + POOL_SUB, lo:lo + POOL_GW] = jnp.dot(
                band_ref[var_idx, gi], u_scr[r0:r0 + POOL_WIN, lo:lo + POOL_GW], preferred_element_type=F32)

    row = lax.broadcasted_iota(jnp.int32, (2 * DK, DV), 0)
    even_row = ((row >> 5) & 1) == 0
    row_even = even_row.astype(F32)
    s_mask = (row_even.astype(BF16), (1.0 - row_even).astype(BF16))
    zeros_v = jnp.zeros((c, DV), BF16)
    order = [(ci, p) for ci in reversed(range(tb // c)) for p in range(PAIRS)]

    def expand(w):
        return jnp.concatenate([w * s_mask[0], w * s_mask[1]], axis=1)

    def early(j):
        ci, p = order[j]
        rows = slice(ci * c, (ci + 1) * c)
        kt = kt_ref[ci, p]
        q = proj_ref[rows, OFF_Q + p * 128:OFF_Q + (p + 1) * 128]
        s = jnp.dot(q, expand(kt), preferred_element_type=F32)
        pm = (s * m_ref[p]).astype(BF16)
        vp = proj_ref[rows, OFF_V + 2 * p * DV:OFF_V + (2 * p + 2) * DV]
        kv = jnp.dot((kt.astype(F32) * zbt_ref[p]).astype(BF16), vp, preferred_element_type=F32)
        qf32 = q.astype(F32)
        lanes = slice(p * 128, (p + 1) * 128)
        q_cross = jnp.concatenate([(qf32 * xif_ref[:, lanes]).astype(BF16),
                                   (qf32 * xib_ref[:, lanes]).astype(BF16)], axis=1)
        return pm, jnp.where(even_row, kv[:, 0:DV], kv[:, DV:2 * DV]), q_cross

    def values(j, early_out):
        pm, kvb, q_cross = early_out
        ci, p = order[j]
        rows = slice(ci * c, (ci + 1) * c)
        lo = p * 128
        v0 = proj_ref[rows, OFF_V + 2 * lo:OFF_V + 2 * lo + DV]
        v1 = proj_ref[rows, OFF_V + 2 * lo + DV:OFF_V + 2 * lo + 2 * DV]
        cur_b = stb[p]
        stb[p] = cur_b * dcb_ref[p] + kvb
        rhs = jnp.concatenate(
            [jnp.concatenate([v0, zeros_v], axis=1),
             jnp.concatenate([zeros_v, v1], axis=1),
             expand(sf_ref[ci, p]),
             expand(cur_b.astype(BF16))], axis=0)
        o2 = jnp.dot(jnp.concatenate([pm, q_cross], axis=1), rhs, preferred_element_type=F32)
        for hh in range(2):
            hl = (2 * p + hh) * DV
            o = o2[:, hh * DV:(hh + 1) * DV]
            mu = jnp.mean(o, axis=-1, keepdims=True)
            d = o - mu
            var = jnp.mean(d * d, axis=-1, keepdims=True)
            a = proj_ref[rows, OFF_A + hl:OFF_A + hl + DV].astype(F32)
            yr_scr[rows, hl:hl + DV] = a * (d * lax.rsqrt(var + EPS))
        if p == PAIRS - 1:
            sgp = proj_ref[rows, OFF_SGP:OFF_SGP + D_MODEL].astype(F32)
            m_scr[slot, rows, :] = (yr_scr[rows, :] + sgp * yp_scr[rows, :]).astype(BF16)

    n_wout = (tb // WOUT_ROWS) * (D_MODEL // WOUT_COLS)
    fillers = {0: lambda: pool_group(0), 1: lambda: pool_group(1), 2: lambda: pool_group(2),
               3: lambda: pool_group(3)}
    for k in range(1, n_wout - 1):
        fillers[2 + 2 * k] = functools.partial(wout_piece, k)
    assert max(fillers) < len(order)

    wout_piece(0)
    pending = {0: early(0), 1: early(1)}
    for j in range(len(order)):
        if j in fillers:
            fillers[j]()
        values(j, pending.pop(j))
        if j + 2 < len(order):
            pending[j + 2] = early(j + 2)
    wout_piece(n_wout - 1)


def _pool_bands():
    i = np.arange(POOL_SUB)[:, None]
    j = np.arange(POOL_WIN)[None, :] - HALO
    out = np.zeros((3, len(POOL_WINDOWS), POOL_SUB, POOL_WIN), np.float64)
    for gi, w in enumerate(POOL_WINDOWS):
        half = w // 2
        inside = ((j - i >= -half) & (j - i <= half - 1)).astype(np.float64)
        token = (j == i).astype(np.float64)
        counts = {
            BAND_INTERIOR: np.full((POOL_SUB, 1), float(w)),
            BAND_FIRST: (i + half) - np.maximum(i - half, 0),
            BAND_LAST: np.minimum(i + half, POOL_SUB) - (i - half),
        }
        for var, cnt in counts.items():
            out[var, gi] = inside / cnt - token
    return out.astype(np.float32)


def _mixer(proj, kt, sf, layer, w, tabs, seq_len):
    t = proj.shape[0]
    tb = TB_MIX
    nb = t // tb
    cpb = tb // CHUNK
    hb = tb // HALO
    n_halo = t // HALO
    u_col = OFF_U // D_MODEL
    xif, xib, zbt, dcb, m_tab = tabs
    band = jnp.asarray(_pool_bands(), BF16)
    rev = lambda i: nb - 1 - jnp.minimum(i, nb - 1)
    rev_out = lambda i: nb - 1 - jnp.maximum(i - 1, 0)
    lay3 = lambda i: (layer, 0, 0)
    lay4 = lambda i: (layer, 0, 0, 0)
    state_blk = pl.BlockSpec((cpb, PAIRS, 2 * DK, DV), lambda i: (rev(i), 0, 0, 0))
    return pl.pallas_call(
        functools.partial(_mixer_kernel, seq_len),
        grid=(nb + 1,),
        in_specs=[
            pl.BlockSpec((tb, D_PROJ), lambda i: (rev(i), 0)),
            pl.BlockSpec((HALO, D_MODEL), lambda i: (jnp.maximum(rev(i) * hb - 1, 0), u_col)),
            pl.BlockSpec((HALO, D_MODEL), lambda i: (jnp.minimum((rev(i) + 1) * hb, n_halo - 1), u_col)),
            state_blk, state_blk,
            pl.BlockSpec((None, CHUNK, D_QK), lay3),
            pl.BlockSpec((None, CHUNK, D_QK), lay3),
            pl.BlockSpec((None, PAIRS, 2 * DK, CHUNK), lay4),
            pl.BlockSpec((None, PAIRS, 2 * DK, DV), lay4),
            pl.BlockSpec((None, PAIRS, CHUNK, 2 * CHUNK), lay4),
            pl.BlockSpec(band.shape, lambda i: (0, 0, 0, 0)),
            pl.BlockSpec((None, D_MODEL, D_MODEL), lay3),
        ],
        out_specs=pl.BlockSpec((tb, D_MODEL), lambda i: (rev_out(i), 0)),
        out_shape=jax.ShapeDtypeStruct((t, D_MODEL), F32),
        scratch_shapes=[
            pltpu.VMEM((tb, D_MODEL), F32),
            pltpu.VMEM((tb, D_MODEL), F32),
            pltpu.VMEM((tb - POOL_SUB + POOL_WIN, D_MODEL), BF16),
            pltpu.VMEM((2, tb, D_MODEL), BF16),
            pltpu.VMEM((PAIRS, 2 * DK, DV), F32),
        ],
        compiler_params=_params(),
    )(proj, proj, proj, kt, sf, xif, xib, zbt, dcb, m_tab, band, w["w_out"])


def _mlp_body(x_ref, z_ref, gmix_ref, gpre_ref, w1_ref, w2_ref, gpost_ref, out_ref, x1_rd, h_rd, x1_wr, h_wr):
    tb = x_ref.shape[0]
    n_ff = D_FF // FF_CHUNK
    rows_per = tb // (2 * n_ff)

    def norms(k):
        rows = slice(k * rows_per, (k + 1) * rows_per)
        x1 = x_ref[rows, :] + _rms(z_ref[rows, :], gmix_ref[...])
        x1_wr[rows, :] = x1
        h_wr[rows, :] = _rms(x1, gpre_ref[...]).astype(BF16)
        return _zero_like_row(x1[0:1, :])

    h = h_rd[...]
    acc = jnp.zeros((tb, D_MODEL), F32)
    for j in range(n_ff):
        lo = j * FF_CHUNK
        floor = jnp.maximum(norms(2 * j), norms(2 * j + 1))[:, 0:FF_CHUNK]
        f = jnp.dot(h, w1_ref[:, lo:lo + FF_CHUNK], preferred_element_type=F32)
        f = jnp.square(jnp.maximum(f, floor)).astype(BF16)
        acc = acc + jnp.dot(f, w2_ref[lo:lo + FF_CHUNK, :], preferred_element_type=F32)
    out_ref[...] = x1_rd[...] + _rms(acc, gpost_ref[...])


def _mlp_kernel(x_ref, z_ref, gmix_ref, gpre_ref, w1_ref, w2_ref, gpost_ref, out_ref, x1_a, h_a, x1_b, h_b):
    step = pl.program_id(0)
    args = (x_ref, z_ref, gmix_ref, gpre_ref, w1_ref, w2_ref, gpost_ref, out_ref)

    @pl.when(step == 0)
    def _():
        x1 = x_ref[...] + _rms(z_ref[...], gmix_ref[...])
        x1_a[...] = x1
        h_a[...] = _rms(x1, gpre_ref[...]).astype(BF16)

    @pl.when((lax.rem(step, 2) == 0) & (step > 0))
    def _():
        _mlp_body(*args, x1_b, h_b, x1_a, h_a)

    @pl.when(lax.rem(step, 2) == 1)
    def _():
        _mlp_body(*args, x1_a, h_a, x1_b, h_b)


def _mlp(x, z, layer, w):
    t = x.shape[0]
    tb = TB_MLP
    nb = t // tb
    lay3 = lambda i: (layer, 0, 0)
    row_l = pl.BlockSpec((None, 1, D_MODEL), lay3)
    cur = lambda i: (jnp.minimum(i, nb - 1), 0)
    return pl.pallas_call(
        _mlp_kernel,
        grid=(nb + 1,),
        in_specs=[
            pl.BlockSpec((tb, D_MODEL), cur),
            pl.BlockSpec((tb, D_MODEL), cur),
            row_l, row_l,
            pl.BlockSpec((None, D_MODEL, D_FF), lay3, pipeline_mode=pl.Buffered(1)),
            pl.BlockSpec((None, D_FF, D_MODEL), lay3, pipeline_mode=pl.Buffered(1)),
            row_l,
        ],
        out_specs=pl.BlockSpec((tb, D_MODEL), lambda i: (jnp.maximum(i - 1, 0), 0)),
        out_shape=jax.ShapeDtypeStruct((t, D_MODEL), F32),
        scratch_shapes=[pltpu.VMEM((tb, D_MODEL), F32), pltpu.VMEM((tb, D_MODEL), BF16),
                        pltpu.VMEM((tb, D_MODEL), F32), pltpu.VMEM((tb, D_MODEL), BF16)],
        compiler_params=_params(),
    )(x, z, w["g_mix_post"], w["g_mlp_pre"], w["w_mlp1"], w["w_mlp2"], w["g_mlp_post"])


def _qk_column_order():
    n = np.arange(D_QK)
    pair, l = n // 128, n % 128
    head = 2 * pair + (l // 32) % 2
    return head * DK + (l // 64) * (DK // 2) + l % 32


def _rope_tables(seq_len):
    half = DK // 2
    inv = ROPE_BASE ** (-jnp.arange(half, dtype=F32) / half)
    ang_i = (jnp.arange(seq_len // CHUNK, dtype=F32) * CHUNK)[:, None] * inv[None, :]
    ang_j = jnp.arange(CHUNK, dtype=F32)[:, None] * inv[None, :]
    ci, si = jnp.cos(ang_i)[:, None, :], jnp.sin(ang_i)[:, None, :]
    cj, sj = jnp.cos(ang_j)[None, :, :], jnp.sin(ang_j)[None, :, :]
    cos = (ci * cj - si * sj).reshape(seq_len, half)
    sin = (si * cj + ci * sj).reshape(seq_len, half)
    return jnp.tile(cos, (1, 4)), jnp.concatenate([-sin, -sin, sin, sin], axis=1)


def _trunk(x, w, tabs, cos_t, sin_t):
    batch, seq_len, d_model = x.shape
    assert d_model == D_MODEL and x.dtype == F32
    assert seq_len % TB_IN == 0 and seq_len % TB_MIX == 0 and (batch * seq_len) % TB_MLP == 0
    xt = x.reshape(batch * seq_len, D_MODEL)
    m_tab, xif, xib, zft, zbt, dcf, dcb = tabs
    for layer in range(DEPTH):
        proj, kt, sf = _inproj(xt, layer, w, (zft, dcf), cos_t, sin_t, seq_len)
        z = _mixer(proj, kt, sf, layer, w, (xif, xib, zbt, dcb, m_tab), seq_len)
        xt = _mlp(xt, z, layer, w)
    return xt.reshape(batch, seq_len, D_MODEL)


def kernel(x_prompt, x_sample, norm_mix_pre, norm_mix_post, w_in, ret_decay_fwd, ret_decay_bwd, ret_gn,
           pool_w, pool_scale, w_out, norm_mlp_pre, norm_mlp_post, w_mlp1, w_mlp2):
    order = _qk_column_order()
    qk_cols = np.concatenate([order, D_QK + order])
    w = dict(
        w_qk=w_in[:, :, :2 * D_QK].astype(BF16)[:, :, qk_cols], w_in=w_in.astype(BF16),
        w_u=_fold_pool(w_in, pool_w, pool_scale), w_out=w_out.astype(BF16),
        w_mlp1=w_mlp1.astype(BF16), w_mlp2=w_mlp2.astype(BF16),
        g_mix_pre=norm_mix_pre[:, None, :], g_mix_post=norm_mix_post[:, None, :], gn=ret_gn[:, None, :],
        g_mlp_pre=norm_mlp_pre[:, None, :], g_mlp_post=norm_mlp_post[:, None, :])
    dec = jnp.concatenate([ret_decay_fwd, ret_decay_bwd], axis=1).astype(F32)
    tabs = _decay_tables(dec)
    cos_t, sin_t = _rope_tables(max(x_prompt.shape[1], x_sample.shape[1]))
    y_prompt = _trunk(x_prompt, w, tabs, cos_t, sin_t)
    y_sample = _trunk(x_sample, w, tabs, cos_t, sin_t)
    return (y_prompt, y_sample)
```

```python
import functools

import numpy as np
import jax
import jax.numpy as jnp
from jax import lax
from jax.experimental import pallas as pl
from jax.experimental.pallas import tpu as pltpu

D_MODEL = 1024
DEPTH = 4
HEADS = 8
DK = 64
DV = 128
PAIRS = HEADS // 2
D_QK = HEADS * DK
CHUNK = 128
ROPE_BASE = 10000.0
POOL_WINDOWS = (2, 4, 8, 16)
POOL_GW = 256
D_FF = 4 * D_MODEL
EPS = 1e-6
QK_SCALE = DK ** -0.5

OFF_U, OFF_V, OFF_A, OFF_SGP, OFF_Q = 0, 1024, 2048, 3072, 4096
D_PROJ = 4608
W_V, W_G, W_U, W_GR, W_GP = 1024, 2048, 3072, 4096, 5120
HALO = 16
POOL_SUB = 128
POOL_WIN = 256
BAND_INTERIOR, BAND_FIRST, BAND_LAST = 0, 1, 2

TB_IN = 512
TB_MIX = 512
WOUT_ROWS, WOUT_COLS = 256, 256
TB_MLP = 512
FF_CHUNK = 512
VMEM_LIMIT_BYTES = 56 * 1024 * 1024

F32 = jnp.float32
BF16 = jnp.bfloat16


def _rms(x, g):
    return x * lax.rsqrt(jnp.mean(x * x, axis=-1, keepdims=True) + EPS) * g


def _zero_like_row(row):
    bits = lax.bitcast_convert_type(row, jnp.uint32)
    return lax.bitcast_convert_type((bits >> 16) >> 16, F32)


def _log_sigmoid(d):
    return jnp.minimum(d, 0.0) - jnp.log1p(jnp.exp(-jnp.abs(d)))


def _params():
    return pltpu.CompilerParams(dimension_semantics=("arbitrary",), vmem_limit_bytes=VMEM_LIMIT_BYTES)


def _tables_kernel(dec_ref, m_ref, xif_ref, xib_ref, zft_ref, zbt_ref, dcf_ref, dcb_ref):
    l = pl.program_id(0)
    c = CHUNK
    df = [dec_ref[l, h] for h in range(HEADS)]
    db = [dec_ref[l, HEADS + h] for h in range(HEADS)]

    ii = lax.broadcasted_iota(jnp.int32, (c, c), 0)
    jj = lax.broadcasted_iota(jnp.int32, (c, c), 1)
    dist = jnp.abs(ii - jj).astype(F32)
    for h in range(HEADS):
        dsel = jnp.where(ii >= jj, jnp.full((c, c), df[h], F32), jnp.full((c, c), db[h], F32))
        m_ref[0, h // 2, :, (h % 2) * c:(h % 2 + 1) * c] = jnp.exp(dist * _log_sigmoid(dsel))

    rows = xif_ref.shape[1]
    lane = lax.broadcasted_iota(jnp.int32, (rows, D_QK), 1)
    head = 2 * (lane >> 7) + ((lane >> 5) & 1)
    idx = (lax.broadcasted_iota(jnp.int32, (rows, D_QK), 0) & (c - 1)).astype(F32)
    sel_f = jnp.full((rows, D_QK), df[0], F32)
    sel_b = jnp.full((rows, D_QK), db[0], F32)
    for h in range(1, HEADS):
        sel_f = jnp.where(head == h, df[h], sel_f)
        sel_b = jnp.where(head == h, db[h], sel_b)
    xif_ref[0] = jnp.exp((idx + 1.0) * _log_sigmoid(sel_f))
    xib_ref[0] = jnp.exp((c - idx) * _log_sigmoid(sel_b))

    rl = lax.broadcasted_iota(jnp.int32, (2 * DK, c), 0)
    odd = ((rl >> 5) & 1) == 1
    jc = lax.broadcasted_iota(jnp.int32, (2 * DK, c), 1).astype(F32)
    for p in range(PAIRS):
        lg_f = _log_sigmoid(jnp.where(odd, df[2 * p + 1], df[2 * p]))
        lg_b = _log_sigmoid(jnp.where(odd, db[2 * p + 1], db[2 * p]))
        zft_ref[0, p] = jnp.exp((c - 1.0 - jc) * lg_f)
        zbt_ref[0, p] = jnp.exp(jc * lg_b)
        dcf_ref[0, p] = jnp.exp(float(c) * lg_f)
        dcb_ref[0, p] = jnp.exp(float(c) * lg_b)


def _decay_tables(dec):
    c = CHUNK
    pair_tab = jax.ShapeDtypeStruct((DEPTH, PAIRS, 2 * DK, c), F32)
    pair_spec = pl.BlockSpec((1, PAIRS, 2 * DK, c), lambda l: (l, 0, 0, 0))
    lane_tab = jax.ShapeDtypeStruct((DEPTH, c, D_QK), F32)
    lane_spec = pl.BlockSpec((1, c, D_QK), lambda l: (l, 0, 0))
    return pl.pallas_call(
        _tables_kernel,
        grid=(DEPTH,),
        in_specs=[pl.BlockSpec(memory_space=pltpu.SMEM)],
        out_specs=(pl.BlockSpec((1, PAIRS, c, 2 * c), lambda l: (l, 0, 0, 0)),
                   lane_spec, lane_spec, pair_spec, pair_spec, pair_spec, pair_spec),
        out_shape=(jax.ShapeDtypeStruct((DEPTH, PAIRS, c, 2 * c), F32),
                   lane_tab, lane_tab, pair_tab, pair_tab, pair_tab, pair_tab),
        compiler_params=_params(),
    )(dec)


def _fold_pool_kernel(wu_ref, pw_ref, ps_ref, out_ref):
    for g in range(len(POOL_WINDOWS)):
        cols = slice(g * POOL_GW, (g + 1) * POOL_GW)
        prod = jnp.dot(wu_ref[:, cols].astype(BF16), pw_ref[g].astype(BF16), preferred_element_type=F32)
        out_ref[:, cols] = (prod * ps_ref[:, cols]).astype(out_ref.dtype)


def _fold_pool(w_in, pool_w, pool_scale):
    n_groups = len(POOL_WINDOWS)
    return pl.pallas_call(
        _fold_pool_kernel,
        grid=(DEPTH,),
        in_specs=[
            pl.BlockSpec((None, D_MODEL, D_MODEL), lambda l: (l, 0, W_U // D_MODEL)),
            pl.BlockSpec((None, n_groups, POOL_GW, POOL_GW), lambda l: (l, 0, 0, 0)),
            pl.BlockSpec((None, 1, D_MODEL), lambda l: (l, 0, 0)),
        ],
        out_specs=pl.BlockSpec((None, D_MODEL, D_MODEL), lambda l: (l, 0, 0)),
        out_shape=jax.ShapeDtypeStruct((DEPTH, D_MODEL, D_MODEL), BF16),
        compiler_params=_params(),
    )(w_in, pool_w, pool_scale[:, None, :])


def _permute_qk_kernel(w_ref, perm_ref, out_ref):
    out_ref[...] = jnp.dot(w_ref[...].astype(BF16), perm_ref[...], preferred_element_type=F32).astype(out_ref.dtype)


def _permute_qk(w_in):
    order = _qk_column_order()
    cols = np.concatenate([order, D_QK + order])
    perm = np.zeros((2 * D_QK, 2 * D_QK), np.float32)
    perm[cols, np.arange(2 * D_QK)] = 1.0
    return pl.pallas_call(
        _permute_qk_kernel,
        grid=(DEPTH,),
        in_specs=[
            pl.BlockSpec((None, D_MODEL, 2 * D_QK), lambda l: (l, 0, 0)),
            pl.BlockSpec((2 * D_QK, 2 * D_QK), lambda l: (0, 0)),
        ],
        out_specs=pl.BlockSpec((None, D_MODEL, 2 * D_QK), lambda l: (l, 0, 0)),
        out_shape=jax.ShapeDtypeStruct((DEPTH, D_MODEL, 2 * D_QK), BF16),
        compiler_params=_params(),
    )(w_in, jnp.asarray(perm, BF16))


def _inproj_kernel(seq_len, x_ref, g_ref, wqk_ref, w_ref, wu_ref, cos_ref, sin_ref,
                   zft_ref, dcf_ref, gn_ref, proj_ref, kt_ref, sf_ref, k_scr, stf):
    tb = x_ref.shape[0]

    @pl.when(lax.rem(pl.program_id(0) * tb, seq_len) == 0)
    def _():
        stf[...] = jnp.zeros_like(stf)

    half = tb // 2
    h_top = _rms(x_ref[0:half, :], g_ref[...]).astype(BF16)
    qk_top = jnp.dot(h_top, wqk_ref[...], preferred_element_type=F32)
    h_bot = _rms(x_ref[half:tb, :], g_ref[...]).astype(BF16)
    qk_bot = jnp.dot(h_bot, wqk_ref[...], preferred_element_type=F32)
    h = jnp.concatenate([h_top, h_bot], axis=0)
    qk = jnp.concatenate([qk_top, qk_bot], axis=0)
    cos = cos_ref[...]
    sin = sin_ref[...]

    for p in range(PAIRS):
        lo = p * 128
        qp = qk[:, lo:lo + 128]
        qr = (qp * cos + pltpu.roll(qp, 64, 1) * sin) * QK_SCALE
        proj_ref[:, OFF_Q + lo:OFF_Q + lo + 128] = qr.astype(BF16)
        kp = qk[:, D_QK + lo:D_QK + lo + 128]
        kr = kp * cos + pltpu.roll(kp, 64, 1) * sin
        k_scr[:, lo:lo + 128] = kr

    def section(w_off):
        return jnp.dot(h, w_ref[:, w_off:w_off + D_MODEL], preferred_element_type=F32)

    v = section(W_V).astype(BF16)
    proj_ref[:, OFF_V:OFF_V + D_MODEL] = v

    rl = lax.broadcasted_iota(jnp.int32, (2 * DK, DV), 0)
    even_row = ((rl >> 5) & 1) == 0

    def summaries(c):
        r0 = c * CHUNK
        for p in range(PAIRS):
            kt = k_scr[r0:r0 + CHUNK, p * 128:(p + 1) * 128].T
            kt_ref[c, p] = kt.astype(BF16)
            vp = v[r0:r0 + CHUNK, 2 * p * DV:(2 * p + 2) * DV]
            kv = jnp.dot((kt * zft_ref[p]).astype(BF16), vp, preferred_element_type=F32)
            cur = stf[p]
            sf_ref[c, p] = cur.astype(BF16)
            stf[p] = cur * dcf_ref[p] + jnp.where(even_row, kv[:, 0:DV], kv[:, DV:2 * DV])

    g = section(W_G)
    a_gate = g * jax.nn.sigmoid(g) * jax.nn.sigmoid(section(W_GR)) * gn_ref[...]
    proj_ref[:, OFF_A:OFF_A + D_MODEL] = a_gate.astype(BF16)
    proj_ref[:, OFF_U:OFF_U + D_MODEL] = jnp.dot(h, wu_ref[...], preferred_element_type=F32).astype(BF16)
    proj_ref[:, OFF_SGP:OFF_SGP + D_MODEL] = jax.nn.sigmoid(section(W_GP)).astype(BF16)
    for c in range(tb // CHUNK):
        summaries(c)


def _inproj(x, layer, w, tabs, cos_t, sin_t, seq_len):
    t = x.shape[0]
    tb = TB_IN
    nb = t // tb
    nb_seq = seq_len // tb
    cpb = tb // CHUNK
    zft, dcf = tabs
    row_l = pl.BlockSpec((None, 1, D_MODEL), lambda i: (layer, 0, 0))
    pair_l = pl.BlockSpec((None, PAIRS, 2 * DK, CHUNK), lambda i: (layer, 0, 0, 0))
    rope_blk = pl.BlockSpec((tb, 128), lambda i: (i % nb_seq, 0))
    state_blk = pl.BlockSpec((cpb, PAIRS, 2 * DK, DV), lambda i: (i, 0, 0, 0))
    return pl.pallas_call(
        functools.partial(_inproj_kernel, seq_len),
        grid=(nb,),
        in_specs=[
            pl.BlockSpec((tb, D_MODEL), lambda i: (i, 0)),
            row_l,
            pl.BlockSpec((None,) + w["w_qk"].shape[1:], lambda i: (layer, 0, 0), pipeline_mode=pl.Buffered(1)),
            pl.BlockSpec((None,) + w["w_in"].shape[1:], lambda i: (layer, 0, 0), pipeline_mode=pl.Buffered(1)),
            pl.BlockSpec((None, D_MODEL, D_MODEL), lambda i: (layer, 0, 0), pipeline_mode=pl.Buffered(1)),
            rope_blk, rope_blk,
            pair_l, pair_l,
            row_l,
        ],
        out_specs=(pl.BlockSpec((tb, D_PROJ), lambda i: (i, 0)), state_blk, state_blk),
        out_shape=(
            jax.ShapeDtypeStruct((t, D_PROJ), BF16),
            jax.ShapeDtypeStruct((t // CHUNK, PAIRS, 2 * DK, CHUNK), BF16),
            jax.ShapeDtypeStruct((t // CHUNK, PAIRS, 2 * DK, DV), BF16),
        ),
        scratch_shapes=[pltpu.VMEM((tb, D_QK), F32), pltpu.VMEM((PAIRS, 2 * DK, DV), F32)],
        compiler_params=_params(),
    )(x, w["g_mix_pre"], w["w_qk"], w["w_in"], w["w_u"], cos_t, sin_t, zft, dcf, w["gn"])


def _mixer_kernel(seq_len, proj_ref, uprev_ref, unext_ref, kt_ref, sf_ref, xif_ref, xib_ref, zbt_ref, dcb_ref,
                  m_ref, band_ref, wout_ref, out_ref, yr_scr, yp_scr, u_scr, m_scr, stb):
    step = pl.program_id(0)
    nblk = pl.num_programs(0) - 1

    @pl.when(step < nblk)
    def _():
        _mixer_step(seq_len, proj_ref, uprev_ref, unext_ref, kt_ref, sf_ref, xif_ref, xib_ref, zbt_ref, dcb_ref,
                    m_ref, band_ref, wout_ref, out_ref, yr_scr, yp_scr, u_scr, m_scr, stb)

    @pl.when(step == nblk)
    def _():
        out_ref[...] = jnp.dot(m_scr[1 - lax.rem(step, 2)], wout_ref[...], preferred_element_type=F32)


def _mixer_step(seq_len, proj_ref, uprev_ref, unext_ref, kt_ref, sf_ref, xif_ref, xib_ref, zbt_ref, dcb_ref,
                m_ref, band_ref, wout_ref, out_ref, yr_scr, yp_scr, u_scr, m_scr, stb):
    tb = proj_ref.shape[0]
    c = CHUNK
    step = pl.program_id(0)
    nblk = pl.num_programs(0) - 1
    blk = nblk - 1 - step
    s0 = lax.rem(blk * tb, seq_len)
    slot = lax.rem(step, 2)

    @pl.when(step == 0)
    def _():
        m_scr[1] = jnp.zeros(m_scr.shape[1:], BF16)

    @pl.when(s0 + tb == seq_len)
    def _():
        stb[...] = jnp.zeros_like(stb)

    def wout_piece(idx):
        rh, nt = divmod(idx, D_MODEL // WOUT_COLS)
        rows = slice(rh * WOUT_ROWS, (rh + 1) * WOUT_ROWS)
        cols = slice(nt * WOUT_COLS, (nt + 1) * WOUT_COLS)
        out_ref[rows, cols] = jnp.dot(m_scr[1 - slot, rows, :], wout_ref[:, cols], preferred_element_type=F32)

    keep_prev = jnp.where(s0 > 0, 1.0, 0.0)
    keep_next = jnp.where(s0 + tb < seq_len, 1.0, 0.0)
    u_scr[0:HALO, :] = (uprev_ref[...].astype(F32) * keep_prev).astype(BF16)
    u_scr[HALO:HALO + tb, :] = proj_ref[:, OFF_U:OFF_U + D_MODEL]
    u_scr[HALO + tb:2 * HALO + tb, :] = (unext_ref[...].astype(F32) * keep_next).astype(BF16)
    u_scr[2 * HALO + tb:, :] = jnp.zeros((u_scr.shape[0] - 2 * HALO - tb, D_MODEL), BF16)
    nsub = tb // POOL_SUB
    first_var = jnp.where(s0 == 0, BAND_FIRST, BAND_INTERIOR)
    last_var = jnp.where(s0 + tb == seq_len, BAND_LAST, BAND_INTERIOR)

    def pool_group(gi):
        lo = gi * POOL_GW
        for sb in range(nsub):
            r0 = sb * POOL_SUB
            var_idx = first_var if sb == 0 else (last_var if sb == nsub - 1 else BAND_INTERIOR)
            yp_scr[r0:r0 + POOL_SUB, lo:lo + POOL_GW] = jnp.dot(
                band_ref[var_idx, gi], u_scr[r0:r0 + POOL_WIN, lo:lo + POOL_GW], preferred_element_type=F32)

    row = lax.broadcasted_iota(jnp.int32, (2 * DK, DV), 0)
    even_row = ((row >> 5) & 1) == 0
    row_even = even_row.astype(F32)
    s_mask = (row_even.astype(BF16), (1.0 - row_even).astype(BF16))
    zeros_v = jnp.zeros((c, DV), BF16)
    order = [(ci, p) for ci in reversed(range(tb // c)) for p in range(PAIRS)]

    def expand(w):
        return jnp.concatenate([w * s_mask[0], w * s_mask[1]], axis=1)

    def early(j):
        ci, p = order[j]
        rows = slice(ci * c, (ci + 1) * c)
        kt = kt_ref[ci, p]
        q = proj_ref[rows, OFF_Q + p * 128:OFF_Q + (p + 1) * 128]
        s = jnp.dot(q, expand(kt), preferred_element_type=F32)
        pm = (s * m_ref[p]).astype(BF16)
        vp = proj_ref[rows, OFF_V + 2 * p * DV:OFF_V + (2 * p + 2) * DV]
        kv = jnp.dot((kt.astype(F32) * zbt_ref[p]).astype(BF16), vp, preferred_element_type=F32)
        qf32 = q.astype(F32)
        lanes = slice(p * 128, (p + 1) * 128)
        q_cross = jnp.concatenate([(qf32 * xif_ref[:, lanes]).astype(BF16),
                                   (qf32 * xib_ref[:, lanes]).astype(BF16)], axis=1)
        return pm, jnp.where(even_row, kv[:, 0:DV], kv[:, DV:2 * DV]), q_cross

    def values(j, early_out):
        pm, kvb, q_cross = early_out
        ci, p = order[j]
        rows = slice(ci * c, (ci + 1) * c)
        lo = p * 128
        v0 = proj_ref[rows, OFF_V + 2 * lo:OFF_V + 2 * lo + DV]
        v1 = proj_ref[rows, OFF_V + 2 * lo + DV:OFF_V + 2 * lo + 2 * DV]
        cur_b = stb[p]
        stb[p] = cur_b * dcb_ref[p] + kvb
        rhs = jnp.concatenate(
            [jnp.concatenate([v0, zeros_v], axis=1),
             jnp.concatenate([zeros_v, v1], axis=1),
             expand(sf_ref[ci, p]),
             expand(cur_b.astype(BF16))], axis=0)
        o2 = jnp.dot(jnp.concatenate([pm, q_cross], axis=1), rhs, preferred_element_type=F32)
        for hh in range(2):
            hl = (2 * p + hh) * DV
            o = o2[:, hh * DV:(hh + 1) * DV]
            mu = jnp.mean(o, axis=-1, keepdims=True)
            d = o - mu
            var = jnp.mean(d * d, axis=-1, keepdims=True)
            a = proj_ref[rows, OFF_A + hl:OFF_A + hl + DV].astype(F32)
            yr_scr[rows, hl:hl + DV] = a * (d * lax.rsqrt(var + EPS))
        if p == PAIRS - 1:
            sgp = proj_ref[rows, OFF_SGP:OFF_SGP + D_MODEL].astype(F32)
            m_scr[slot, rows, :] = (yr_scr[rows, :] + sgp * yp_scr[rows, :]).astype(BF16)

    n_wout = (tb // WOUT_ROWS) * (D_MODEL // WOUT_COLS)
    fillers = {0: lambda: pool_group(0), 1: lambda: pool_group(1), 2: lambda: pool_group(2),
               3: lambda: pool_group(3)}
    for k in range(1, n_wout - 1):
        fillers[2 + 2 * k] = functools.partial(wout_piece, k)
    assert max(fillers) < len(order)

    wout_piece(0)
    pending = {0: early(0), 1: early(1)}
    for j in range(len(order)):
        if j in fillers:
            fillers[j]()
        values(j, pending.pop(j))
        if j + 2 < len(order):
            pending[j + 2] = early(j + 2)
    wout_piece(n_wout - 1)


def _pool_bands():
    i = np.arange(POOL_SUB)[:, None]
    j = np.arange(POOL_WIN)[None, :] - HALO
    out = np.zeros((3, len(POOL_WINDOWS), POOL_SUB, POOL_WIN), np.float64)
    for gi, w in enumerate(POOL_WINDOWS):
        half = w // 2
        inside = ((j - i >= -half) & (j - i <= half - 1)).astype(np.float64)
        token = (j == i).astype(np.float64)
        counts = {
            BAND_INTERIOR: np.full((POOL_SUB, 1), float(w)),
            BAND_FIRST: (i + half) - np.maximum(i - half, 0),
            BAND_LAST: np.minimum(i + half, POOL_SUB) - (i - half),
        }
        for var, cnt in counts.items():
            out[var, gi] = inside / cnt - token
    return out.astype(np.float32)


def _mixer(proj, kt, sf, layer, w, tabs, seq_len):
    t = proj.shape[0]
    tb = TB_MIX
    nb = t // tb
    cpb = tb // CHUNK
    hb = tb // HALO
    n_halo = t // HALO
    u_col = OFF_U // D_MODEL
    xif, xib, zbt, dcb, m_tab = tabs
    band = jnp.asarray(_pool_bands(), BF16)
    rev = lambda i: nb - 1 - jnp.minimum(i, nb - 1)
    rev_out = lambda i: nb - 1 - jnp.maximum(i - 1, 0)
    lay3 = lambda i: (layer, 0, 0)
    lay4 = lambda i: (layer, 0, 0, 0)
    state_blk = pl.BlockSpec((cpb, PAIRS, 2 * DK, DV), lambda i: (rev(i), 0, 0, 0))
    return pl.pallas_call(
        functools.partial(_mixer_kernel, seq_len),
        grid=(nb + 1,),
        in_specs=[
            pl.BlockSpec((tb, D_PROJ), lambda i: (rev(i), 0)),
            pl.BlockSpec((HALO, D_MODEL), lambda i: (jnp.maximum(rev(i) * hb - 1, 0), u_col)),
            pl.BlockSpec((HALO, D_MODEL), lambda i: (jnp.minimum((rev(i) + 1) * hb, n_halo - 1), u_col)),
            state_blk, state_blk,
            pl.BlockSpec((None, CHUNK, D_QK), lay3),
            pl.BlockSpec((None, CHUNK, D_QK), lay3),
            pl.BlockSpec((None, PAIRS, 2 * DK, CHUNK), lay4),
            pl.BlockSpec((None, PAIRS, 2 * DK, DV), lay4),
            pl.BlockSpec((None, PAIRS, CHUNK, 2 * CHUNK), lay4),
            pl.BlockSpec(band.shape, lambda i: (0, 0, 0, 0)),
            pl.BlockSpec((None, D_MODEL, D_MODEL), lay3),
        ],
        out_specs=pl.BlockSpec((tb, D_MODEL), lambda i: (rev_out(i), 0)),
        out_shape=jax.ShapeDtypeStruct((t, D_MODEL), F32),
        scratch_shapes=[
            pltpu.VMEM((tb, D_MODEL), F32),
            pltpu.VMEM((tb, D_MODEL), F32),
            pltpu.VMEM((tb - POOL_SUB + POOL_WIN, D_MODEL), BF16),
            pltpu.VMEM((2, tb, D_MODEL), BF16),
            pltpu.VMEM((PAIRS, 2 * DK, DV), F32),
        ],
        compiler_params=_params(),
    )(proj, proj, proj, kt, sf, xif, xib, zbt, dcb, m_tab, band, w["w_out"])


def _mlp_body(x_ref, z_ref, gmix_ref, gpre_ref, w1_ref, w2_ref, gpost_ref, out_ref, x1_rd, h_rd, x1_wr, h_wr):
    tb = x_ref.shape[0]
    n_ff = D_FF // FF_CHUNK
    rows_per = tb // (2 * n_ff)

    def norms(k):
        rows = slice(k * rows_per, (k + 1) * rows_per)
        x1 = x_ref[rows, :] + _rms(z_ref[rows, :], gmix_ref[...])
        x1_wr[rows, :] = x1
        h_wr[rows, :] = _rms(x1, gpre_ref[...]).astype(BF16)
        return _zero_like_row(x1[0:1, :])

    h = h_rd[...]
    acc = jnp.zeros((tb, D_MODEL), F32)
    for j in range(n_ff):
        lo = j * FF_CHUNK
        floor = jnp.maximum(norms(2 * j), norms(2 * j + 1))[:, 0:FF_CHUNK]
        f = jnp.dot(h, w1_ref[:, lo:lo + FF_CHUNK], preferred_element_type=F32)
        f = jnp.square(jnp.maximum(f, floor)).astype(BF16)
        acc = acc + jnp.dot(f, w2_ref[lo:lo + FF_CHUNK, :], preferred_element_type=F32)
    out_ref[...] = x1_rd[...] + _rms(acc, gpost_ref[...])


def _mlp_kernel(x_ref, z_ref, gmix_ref, gpre_ref, w1_ref, w2_ref, gpost_ref, out_ref, x1_a, h_a, x1_b, h_b):
    step = pl.program_id(0)
    args = (x_ref, z_ref, gmix_ref, gpre_ref, w1_ref, w2_ref, gpost_ref, out_ref)

    @pl.when(step == 0)
    def _():
        x1 = x_ref[...] + _rms(z_ref[...], gmix_ref[...])
        x1_a[...] = x1
        h_a[...] = _rms(x1, gpre_ref[...]).astype(BF16)

    @pl.when((lax.rem(step, 2) == 0) & (step > 0))
    def _():
        _mlp_body(*args, x1_b, h_b, x1_a, h_a)

    @pl.when(lax.rem(step, 2) == 1)
    def _():
        _mlp_body(*args, x1_a, h_a, x1_b, h_b)


def _mlp(x, z, layer, w):
    t = x.shape[0]
    tb = TB_MLP
    nb = t // tb
    lay3 = lambda i: (layer, 0, 0)
    row_l = pl.BlockSpec((None, 1, D_MODEL), lay3)
    cur = lambda i: (jnp.minimum(i, nb - 1), 0)
    return pl.pallas_call(
        _mlp_kernel,
        grid=(nb + 1,),
        in_specs=[
            pl.BlockSpec((tb, D_MODEL), cur),
            pl.BlockSpec((tb, D_MODEL), cur),
            row_l, row_l,
            pl.BlockSpec((None, D_MODEL, D_FF), lay3, pipeline_mode=pl.Buffered(1)),
            pl.BlockSpec((None, D_FF, D_MODEL), lay3, pipeline_mode=pl.Buffered(1)),
            row_l,
        ],
        out_specs=pl.BlockSpec((tb, D_MODEL), lambda i: (jnp.maximum(i - 1, 0), 0)),
        out_shape=jax.ShapeDtypeStruct((t, D_MODEL), F32),
        scratch_shapes=[pltpu.VMEM((tb, D_MODEL), F32), pltpu.VMEM((tb, D_MODEL), BF16),
                        pltpu.VMEM((tb, D_MODEL), F32), pltpu.VMEM((tb, D_MODEL), BF16)],
        compiler_params=_params(),
    )(x, z, w["g_mix_post"], w["g_mlp_pre"], w["w_mlp1"], w["w_mlp2"], w["g_mlp_post"])


def _qk_column_order():
    n = np.arange(D_QK)
    pair, l = n // 128, n % 128
    head = 2 * pair + (l // 32) % 2
    return head * DK + (l // 64) * (DK // 2) + l % 32


def _rope_tables(seq_len):
    half = DK // 2
    inv = ROPE_BASE ** (-jnp.arange(half, dtype=F32) / half)
    ang_i = (jnp.arange(seq_len // CHUNK, dtype=F32) * CHUNK)[:, None] * inv[None, :]
    ang_j = jnp.arange(CHUNK, dtype=F32)[:, None] * inv[None, :]
    ci, si = jnp.cos(ang_i)[:, None, :], jnp.sin(ang_i)[:, None, :]
    cj, sj = jnp.cos(ang_j)[None, :, :], jnp.sin(ang_j)[None, :, :]
    cos = (ci * cj - si * sj).reshape(seq_len, half)
    sin = (si * cj + ci * sj).reshape(seq_len, half)
    return jnp.tile(cos, (1, 4)), jnp.concatenate([-sin, -sin, sin, sin], axis=1)


def _trunk(x, w, tabs, cos_t, sin_t):
    batch, seq_len, d_model = x.shape
    assert d_model == D_MODEL and x.dtype == F32
    assert seq_len % TB_IN == 0 and seq_len % TB_MIX == 0 and (batch * seq_len) % TB_MLP == 0
    xt = x.reshape(batch * seq_len, D_MODEL)
    m_tab, xif, xib, zft, zbt, dcf, dcb = tabs
    for layer in range(DEPTH):
        proj, kt, sf = _inproj(xt, layer, w, (zft, dcf), cos_t, sin_t, seq_len)
        z = _mixer(proj, kt, sf, layer, w, (xif, xib, zbt, dcb, m_tab), seq_len)
        xt = _mlp(xt, z, layer, w)
    return xt.reshape(batch, seq_len, D_MODEL)


def kernel(x_prompt, x_sample, norm_mix_pre, norm_mix_post, w_in, ret_decay_fwd, ret_decay_bwd, ret_gn,
           pool_w, pool_scale, w_out, norm_mlp_pre, norm_mlp_post, w_mlp1, w_mlp2):
    w = dict(
        w_qk=_permute_qk(w_in), w_in=w_in.astype(BF16),
        w_u=_fold_pool(w_in, pool_w, pool_scale), w_out=w_out.astype(BF16),
        w_mlp1=w_mlp1.astype(BF16), w_mlp2=w_mlp2.astype(BF16),
        g_mix_pre=norm_mix_pre[:, None, :], g_mix_post=norm_mix_post[:, None, :], gn=ret_gn[:, None, :],
        g_mlp_pre=norm_mlp_pre[:, None, :], g_mlp_post=norm_mlp_post[:, None, :])
    dec = jnp.concatenate([ret_decay_fwd, ret_decay_bwd], axis=1).astype(F32)
    tabs = _decay_tables(dec)
    cos_t, sin_t = _rope_tables(max(x_prompt.shape[1], x_sample.shape[1]))
    y_prompt = _trunk(x_prompt, w, tabs, cos_t, sin_t)
    y_sample = _trunk(x_sample, w, tabs, cos_t, sin_t)
    return (y_prompt, y_sample)
```

```python
import functools

import numpy as np
import jax
import jax.numpy as jnp
from jax import lax
from jax.experimental import pallas as pl
from jax.experimental.pallas import tpu as pltpu

D_MODEL = 1024
DEPTH = 4
HEADS = 8
DK = 64
DV = 128
PAIRS = HEADS // 2
D_QK = HEADS * DK
CHUNK = 128
ROPE_BASE = 10000.0
POOL_WINDOWS = (2, 4, 8, 16)
POOL_GW = 256
D_FF = 4 * D_MODEL
EPS = 1e-6
QK_SCALE = DK ** -0.5

OFF_U, OFF_V, OFF_A, OFF_SGP, OFF_Q = 0, 1024, 2048, 3072, 4096
D_PROJ = 4608
W_V, W_G, W_U, W_GR, W_GP = 1024, 2048, 3072, 4096, 5120
HALO = 16
POOL_SUB = 128
POOL_WIN = 256
BAND_INTERIOR, BAND_FIRST, BAND_LAST = 0, 1, 2

TB_IN = 512
TB_MIX = 512
WOUT_ROWS, WOUT_COLS = 256, 256
TB_MLP = 512
FF_CHUNK = 512
VMEM_LIMIT_BYTES = 56 * 1024 * 1024

F32 = jnp.float32
BF16 = jnp.bfloat16


def _rms(x, g):
    return x * lax.rsqrt(jnp.mean(x * x, axis=-1, keepdims=True) + EPS) * g


def _zero_like_row(row):
    bits = lax.bitcast_convert_type(row, jnp.uint32)
    return lax.bitcast_convert_type((bits >> 16) >> 16, F32)


def _log_sigmoid(d):
    return jnp.minimum(d, 0.0) - jnp.log1p(jnp.exp(-jnp.abs(d)))


def _params():
    return pltpu.CompilerParams(dimension_semantics=("arbitrary",), vmem_limit_bytes=VMEM_LIMIT_BYTES)


def _tables_kernel(dec_ref, m_ref, xif_ref, xib_ref, zft_ref, zbt_ref, dcf_ref, dcb_ref):
    l = pl.program_id(0)
    c = CHUNK
    df = [dec_ref[l, h] for h in range(HEADS)]
    db = [dec_ref[l, HEADS + h] for h in range(HEADS)]

    ii = lax.broadcasted_iota(jnp.int32, (c, c), 0)
    jj = lax.broadcasted_iota(jnp.int32, (c, c), 1)
    dist = jnp.abs(ii - jj).astype(F32)
    for h in range(HEADS):
        dsel = jnp.where(ii >= jj, jnp.full((c, c), df[h], F32), jnp.full((c, c), db[h], F32))
        m_ref[0, h // 2, :, (h % 2) * c:(h % 2 + 1) * c] = jnp.exp(dist * _log_sigmoid(dsel))

    rows = xif_ref.shape[1]
    lane = lax.broadcasted_iota(jnp.int32, (rows, D_QK), 1)
    head = 2 * (lane >> 7) + ((lane >> 5) & 1)
    idx = (lax.broadcasted_iota(jnp.int32, (rows, D_QK), 0) & (c - 1)).astype(F32)
    sel_f = jnp.full((rows, D_QK), df[0], F32)
    sel_b = jnp.full((rows, D_QK), db[0], F32)
    for h in range(1, HEADS):
        sel_f = jnp.where(head == h, df[h], sel_f)
        sel_b = jnp.where(head == h, db[h], sel_b)
    xif_ref[0] = jnp.exp((idx + 1.0) * _log_sigmoid(sel_f))
    xib_ref[0] = jnp.exp((c - idx) * _log_sigmoid(sel_b))

    rl = lax.broadcasted_iota(jnp.int32, (2 * DK, c), 0)
    odd = ((rl >> 5) & 1) == 1
    jc = lax.broadcasted_iota(jnp.int32, (2 * DK, c), 1).astype(F32)
    for p in range(PAIRS):
        lg_f = _log_sigmoid(jnp.where(odd, df[2 * p + 1], df[2 * p]))
        lg_b = _log_sigmoid(jnp.where(odd, db[2 * p + 1], db[2 * p]))
        zft_ref[0, p] = jnp.exp((c - 1.0 - jc) * lg_f)
        zbt_ref[0, p] = jnp.exp(jc * lg_b)
        dcf_ref[0, p] = jnp.exp(float(c) * lg_f)
        dcb_ref[0, p] = jnp.exp(float(c) * lg_b)


def _decay_tables(dec):
    c = CHUNK
    pair_tab = jax.ShapeDtypeStruct((DEPTH, PAIRS, 2 * DK, c), F32)
    pair_spec = pl.BlockSpec((1, PAIRS, 2 * DK, c), lambda l: (l, 0, 0, 0))
    lane_tab = jax.ShapeDtypeStruct((DEPTH, c, D_QK), F32)
    lane_spec = pl.BlockSpec((1, c, D_QK), lambda l: (l, 0, 0))
    return pl.pallas_call(
        _tables_kernel,
        grid=(DEPTH,),
        in_specs=[pl.BlockSpec(memory_space=pltpu.SMEM)],
        out_specs=(pl.BlockSpec((1, PAIRS, c, 2 * c), lambda l: (l, 0, 0, 0)),
                   lane_spec, lane_spec, pair_spec, pair_spec, pair_spec, pair_spec),
        out_shape=(jax.ShapeDtypeStruct((DEPTH, PAIRS, c, 2 * c), F32),
                   lane_tab, lane_tab, pair_tab, pair_tab, pair_tab, pair_tab),
        compiler_params=_params(),
    )(dec)


def _fold_pool_kernel(wu_ref, pw_ref, ps_ref, out_ref):
    for g in range(len(POOL_WINDOWS)):
        cols = slice(g * POOL_GW, (g + 1) * POOL_GW)
        prod = jnp.dot(wu_ref[:, cols].astype(BF16), pw_ref[g].astype(BF16), preferred_element_type=F32)
        out_ref[:, cols] = (prod * ps_ref[:, cols]).astype(out_ref.dtype)


def _fold_pool(w_in, pool_w, pool_scale):
    n_groups = len(POOL_WINDOWS)
    return pl.pallas_call(
        _fold_pool_kernel,
        grid=(DEPTH,),
        in_specs=[
            pl.BlockSpec((None, D_MODEL, D_MODEL), lambda l: (l, 0, W_U // D_MODEL)),
            pl.BlockSpec((None, n_groups, POOL_GW, POOL_GW), lambda l: (l, 0, 0, 0)),
            pl.BlockSpec((None, 1, D_MODEL), lambda l: (l, 0, 0)),
        ],
        out_specs=pl.BlockSpec((None, D_MODEL, D_MODEL), lambda l: (l, 0, 0)),
        out_shape=jax.ShapeDtypeStruct((DEPTH, D_MODEL, D_MODEL), BF16),
        compiler_params=_params(),
    )(w_in, pool_w, pool_scale[:, None, :])


def _permute_qk_kernel(w_ref, perm_ref, out_ref):
    out_ref[...] = jnp.dot(w_ref[...].astype(BF16), perm_ref[...], preferred_element_type=F32).astype(out_ref.dtype)


def _permute_qk(w_in):
    order = _qk_column_order()
    cols = np.concatenate([order, D_QK + order])
    perm = np.zeros((2 * D_QK, 2 * D_QK), np.float32)
    perm[cols, np.arange(2 * D_QK)] = 1.0
    return pl.pallas_call(
        _permute_qk_kernel,
        grid=(DEPTH,),
        in_specs=[
            pl.BlockSpec((None, D_MODEL, 2 * D_QK), lambda l: (l, 0, 0)),
            pl.BlockSpec((2 * D_QK, 2 * D_QK), lambda l: (0, 0)),
        ],
        out_specs=pl.BlockSpec((None, D_MODEL, 2 * D_QK), lambda l: (l, 0, 0)),
        out_shape=jax.ShapeDtypeStruct((DEPTH, D_MODEL, 2 * D_QK), BF16),
        compiler_params=_params(),
    )(w_in, jnp.asarray(perm, BF16))


def _inproj_kernel(seq_len, x_ref, g_ref, wqk_ref, w_ref, wu_ref, cos_ref, sin_ref,
                   zft_ref, dcf_ref, gn_ref, proj_ref, kt_ref, sf_ref, k_scr, stf):
    tb = x_ref.shape[0]

    @pl.when(lax.rem(pl.program_id(0) * tb, seq_len) == 0)
    def _():
        stf[...] = jnp.zeros_like(stf)

    half = tb // 2
    h_top = _rms(x_ref[0:half, :], g_ref[...]).astype(BF16)
    qk_top = jnp.dot(h_top, wqk_ref[...], preferred_element_type=F32)
    h_bot = _rms(x_ref[half:tb, :], g_ref[...]).astype(BF16)
    qk_bot = jnp.dot(h_bot, wqk_ref[...], preferred_element_type=F32)
    h = jnp.concatenate([h_top, h_bot], axis=0)
    qk = jnp.concatenate([qk_top, qk_bot], axis=0)
    cos = cos_ref[...]
    sin = sin_ref[...]

    for p in range(PAIRS):
        lo = p * 128
        qp = qk[:, lo:lo + 128]
        qr = (qp * cos + pltpu.roll(qp, 64, 1) * sin) * QK_SCALE
        proj_ref[:, OFF_Q + lo:OFF_Q + lo + 128] = qr.astype(BF16)
        kp = qk[:, D_QK + lo:D_QK + lo + 128]
        kr = kp * cos + pltpu.roll(kp, 64, 1) * sin
        k_scr[:, lo:lo + 128] = kr

    def section(w_off):
        return jnp.dot(h, w_ref[:, w_off:w_off + D_MODEL], preferred_element_type=F32)

    v = section(W_V).astype(BF16)
    proj_ref[:, OFF_V:OFF_V + D_MODEL] = v

    rl = lax.broadcasted_iota(jnp.int32, (2 * DK, DV), 0)
    even_row = ((rl >> 5) & 1) == 0

    def summaries(c):
        r0 = c * CHUNK
        for p in range(PAIRS):
            kt = k_scr[r0:r0 + CHUNK, p * 128:(p + 1) * 128].T
            kt_ref[c, p] = kt.astype(BF16)
            vp = v[r0:r0 + CHUNK, 2 * p * DV:(2 * p + 2) * DV]
            kv = jnp.dot((kt * zft_ref[p]).astype(BF16), vp, preferred_element_type=F32)
            cur = stf[p]
            sf_ref[c, p] = cur.astype(BF16)
            stf[p] = cur * dcf_ref[p] + jnp.where(even_row, kv[:, 0:DV], kv[:, DV:2 * DV])

    g = section(W_G)
    a_gate = g * jax.nn.sigmoid(g) * jax.nn.sigmoid(section(W_GR)) * gn_ref[...]
    proj_ref[:, OFF_A:OFF_A + D_MODEL] = a_gate.astype(BF16)
    proj_ref[:, OFF_U:OFF_U + D_MODEL] = jnp.dot(h, wu_ref[...], preferred_element_type=F32).astype(BF16)
    proj_ref[:, OFF_SGP:OFF_SGP + D_MODEL] = jax.nn.sigmoid(section(W_GP)).astype(BF16)
    for c in range(tb // CHUNK):
        summaries(c)


def _inproj(x, layer, w, tabs, cos_t, sin_t, seq_len):
    t = x.shape[0]
    tb = TB_IN
    nb = t // tb
    nb_seq = seq_len // tb
    cpb = tb // CHUNK
    zft, dcf = tabs
    row_l = pl.BlockSpec((None, 1, D_MODEL), lambda i: (layer, 0, 0))
    pair_l = pl.BlockSpec((None, PAIRS, 2 * DK, CHUNK), lambda i: (layer, 0, 0, 0))
    rope_blk = pl.BlockSpec((tb, 128), lambda i: (i % nb_seq, 0))
    state_blk = pl.BlockSpec((cpb, PAIRS, 2 * DK, DV), lambda i: (i, 0, 0, 0))
    return pl.pallas_call(
        functools.partial(_inproj_kernel, seq_len),
        grid=(nb,),
        in_specs=[
            pl.BlockSpec((tb, D_MODEL), lambda i: (i, 0)),
            row_l,
            pl.BlockSpec((None,) + w["w_qk"].shape[1:], lambda i: (layer, 0, 0), pipeline_mode=pl.Buffered(1)),
            pl.BlockSpec((None,) + w["w_in"].shape[1:], lambda i: (layer, 0, 0), pipeline_mode=pl.Buffered(1)),
            pl.BlockSpec((None, D_MODEL, D_MODEL), lambda i: (layer, 0, 0), pipeline_mode=pl.Buffered(1)),
            rope_blk, rope_blk,
            pair_l, pair_l,
            row_l,
        ],
        out_specs=(pl.BlockSpec((tb, D_PROJ), lambda i: (i, 0)), state_blk, state_blk),
        out_shape=(
            jax.ShapeDtypeStruct((t, D_PROJ), BF16),
            jax.ShapeDtypeStruct((t // CHUNK, PAIRS, 2 * DK, CHUNK), BF16),
            jax.ShapeDtypeStruct((t // CHUNK, PAIRS, 2 * DK, DV), BF16),
        ),
        scratch_shapes=[pltpu.VMEM((tb, D_QK), F32), pltpu.VMEM((PAIRS, 2 * DK, DV), F32)],
        compiler_params=_params(),
    )(x, w["g_mix_pre"], w["w_qk"], w["w_in"], w["w_u"], cos_t, sin_t, zft, dcf, w["gn"])


def _mixer_kernel(seq_len, proj_ref, uprev_ref, unext_ref, kt_ref, sf_ref, xif_ref, xib_ref, zbt_ref, dcb_ref,
                  m_ref, band_ref, wout_ref, out_ref, yr_scr, yp_scr, u_scr, m_scr, stb):
    step = pl.program_id(0)
    nblk = pl.num_programs(0) - 1

    @pl.when(step < nblk)
    def _():
        _mixer_step(seq_len, proj_ref, uprev_ref, unext_ref, kt_ref, sf_ref, xif_ref, xib_ref, zbt_ref, dcb_ref,
                    m_ref, band_ref, wout_ref, out_ref, yr_scr, yp_scr, u_scr, m_scr, stb)

    @pl.when(step == nblk)
    def _():
        out_ref[...] = jnp.dot(m_scr[1 - lax.rem(step, 2)], wout_ref[...], preferred_element_type=F32)


def _mixer_step(seq_len, proj_ref, uprev_ref, unext_ref, kt_ref, sf_ref, xif_ref, xib_ref, zbt_ref, dcb_ref,
                m_ref, band_ref, wout_ref, out_ref, yr_scr, yp_scr, u_scr, m_scr, stb):
    tb = proj_ref.shape[0]
    c = CHUNK
    step = pl.program_id(0)
    nblk = pl.num_programs(0) - 1
    blk = nblk - 1 - step
    s0 = lax.rem(blk * tb, seq_len)
    slot = lax.rem(step, 2)

    @pl.when(step == 0)
    def _():
        m_scr[1] = jnp.zeros(m_scr.shape[1:], BF16)

    @pl.when(s0 + tb == seq_len)
    def _():
        stb[...] = jnp.zeros_like(stb)

    def wout_piece(idx):
        rh, nt = divmod(idx, D_MODEL // WOUT_COLS)
        rows = slice(rh * WOUT_ROWS, (rh + 1) * WOUT_ROWS)
        cols = slice(nt * WOUT_COLS, (nt + 1) * WOUT_COLS)
        out_ref[rows, cols] = jnp.dot(m_scr[1 - slot, rows, :], wout_ref[:, cols], preferred_element_type=F32)

    keep_prev = jnp.where(s0 > 0, 1.0, 0.0)
    keep_next = jnp.where(s0 + tb < seq_len, 1.0, 0.0)
    u_scr[0:HALO, :] = (uprev_ref[...].astype(F32) * keep_prev).astype(BF16)
    u_scr[HALO:HALO + tb, :] = proj_ref[:, OFF_U:OFF_U + D_MODEL]
    u_scr[HALO + tb:2 * HALO + tb, :] = (unext_ref[...].astype(F32) * keep_next).astype(BF16)
    u_scr[2 * HALO + tb:, :] = jnp.zeros((u_scr.shape[0] - 2 * HALO - tb, D_MODEL), BF16)
    nsub = tb // POOL_SUB
    first_var = jnp.where(s0 == 0, BAND_FIRST, BAND_INTERIOR)
    last_var = jnp.where(s0 + tb == seq_len, BAND_LAST, BAND_INTERIOR)

    def pool_group(gi):
        lo = gi * POOL_GW
        for sb in range(nsub):
            r0 = sb * POOL_SUB
            var_idx = first_var if sb == 0 else (last_var if sb == nsub - 1 else BAND_INTERIOR)
            yp_scr[r0:r0 + POOL_SUB, lo:lo + POOL_GW] = jnp.dot(
                band_ref[var_idx, gi].astype(BF16), u_scr[r0:r0 + POOL_WIN, lo:lo + POOL_GW],
                preferred_element_type=F32)

    row = lax.broadcasted_iota(jnp.int32, (2 * DK, DV), 0)
    even_row = ((row >> 5) & 1) == 0
    row_even = even_row.astype(F32)
    s_mask = (row_even.astype(BF16), (1.0 - row_even).astype(BF16))
    zeros_v = jnp.zeros((c, DV), BF16)
    order = [(ci, p) for ci in reversed(range(tb // c)) for p in range(PAIRS)]

    def expand(w):
        return jnp.concatenate([w * s_mask[0], w * s_mask[1]], axis=1)

    def early(j):
        ci, p = order[j]
        rows = slice(ci * c, (ci + 1) * c)
        kt = kt_ref[ci, p]
        q = proj_ref[rows, OFF_Q + p * 128:OFF_Q + (p + 1) * 128]
        s = jnp.dot(q, expand(kt), preferred_element_type=F32)
        pm = (s * m_ref[p]).astype(BF16)
        vp = proj_ref[rows, OFF_V + 2 * p * DV:OFF_V + (2 * p + 2) * DV]
        kv = jnp.dot((kt.astype(F32) * zbt_ref[p]).astype(BF16), vp, preferred_element_type=F32)
        qf32 = q.astype(F32)
        lanes = slice(p * 128, (p + 1) * 128)
        q_cross = jnp.concatenate([(qf32 * xif_ref[:, lanes]).astype(BF16),
                                   (qf32 * xib_ref[:, lanes]).astype(BF16)], axis=1)
        return pm, jnp.where(even_row, kv[:, 0:DV], kv[:, DV:2 * DV]), q_cross

    def values(j, early_out):
        pm, kvb, q_cross = early_out
        ci, p = order[j]
        rows = slice(ci * c, (ci + 1) * c)
        lo = p * 128
        v0 = proj_ref[rows, OFF_V + 2 * lo:OFF_V + 2 * lo + DV]
        v1 = proj_ref[rows, OFF_V + 2 * lo + DV:OFF_V + 2 * lo + 2 * DV]
        cur_b = stb[p]
        stb[p] = cur_b * dcb_ref[p] + kvb
        rhs = jnp.concatenate(
            [jnp.concatenate([v0, zeros_v], axis=1),
             jnp.concatenate([zeros_v, v1], axis=1),
             expand(sf_ref[ci, p]),
             expand(cur_b.astype(BF16))], axis=0)
        o2 = jnp.dot(jnp.concatenate([pm, q_cross], axis=1), rhs, preferred_element_type=F32)
        for hh in range(2):
            hl = (2 * p + hh) * DV
            o = o2[:, hh * DV:(hh + 1) * DV]
            mu = jnp.mean(o, axis=-1, keepdims=True)
            d = o - mu
            var = jnp.mean(d * d, axis=-1, keepdims=True)
            a = proj_ref[rows, OFF_A + hl:OFF_A + hl + DV].astype(F32)
            yr_scr[rows, hl:hl + DV] = a * (d * lax.rsqrt(var + EPS))
        if p == PAIRS - 1:
            sgp = proj_ref[rows, OFF_SGP:OFF_SGP + D_MODEL].astype(F32)
            m_scr[slot, rows, :] = (yr_scr[rows, :] + sgp * yp_scr[rows, :]).astype(BF16)

    n_wout = (tb // WOUT_ROWS) * (D_MODEL // WOUT_COLS)
    fillers = {0: lambda: pool_group(0), 1: lambda: pool_group(1), 2: lambda: pool_group(2),
               3: lambda: pool_group(3)}
    for k in range(1, n_wout - 1):
        fillers[2 + 2 * k] = functools.partial(wout_piece, k)
    assert max(fillers) < len(order)

    wout_piece(0)
    pending = {0: early(0), 1: early(1)}
    for j in range(len(order)):
        if j in fillers:
            fillers[j]()
        values(j, pending.pop(j))
        if j + 2 < len(order):
            pending[j + 2] = early(j + 2)
    wout_piece(n_wout - 1)


def _pool_bands():
    i = np.arange(POOL_SUB)[:, None]
    j = np.arange(POOL_WIN)[None, :] - HALO
    out = np.zeros((3, len(POOL_WINDOWS), POOL_SUB, POOL_WIN), np.float64)
    for gi, w in enumerate(POOL_WINDOWS):
        half = w // 2
        inside = ((j - i >= -half) & (j - i <= half - 1)).astype(np.float64)
        token = (j == i).astype(np.float64)
        counts = {
            BAND_INTERIOR: np.full((POOL_SUB, 1), float(w)),
            BAND_FIRST: (i + half) - np.maximum(i - half, 0),
            BAND_LAST: np.minimum(i + half, POOL_SUB) - (i - half),
        }
        for var, cnt in counts.items():
            out[var, gi] = inside / cnt - token
    return out.astype(np.float32)


def _mixer(proj, kt, sf, layer, w, tabs, seq_len):
    t = proj.shape[0]
    tb = TB_MIX
    nb = t // tb
    cpb = tb // CHUNK
    hb = tb // HALO
    n_halo = t // HALO
    u_col = OFF_U // D_MODEL
    xif, xib, zbt, dcb, m_tab = tabs
    band = jnp.asarray(_pool_bands(), F32)
    rev = lambda i: nb - 1 - jnp.minimum(i, nb - 1)
    rev_out = lambda i: nb - 1 - jnp.maximum(i - 1, 0)
    lay3 = lambda i: (layer, 0, 0)
    lay4 = lambda i: (layer, 0, 0, 0)
    state_blk = pl.BlockSpec((cpb, PAIRS, 2 * DK, DV), lambda i: (rev(i), 0, 0, 0))
    return pl.pallas_call(
        functools.partial(_mixer_kernel, seq_len),
        grid=(nb + 1,),
        in_specs=[
            pl.BlockSpec((tb, D_PROJ), lambda i: (rev(i), 0)),
            pl.BlockSpec((HALO, D_MODEL), lambda i: (jnp.maximum(rev(i) * hb - 1, 0), u_col)),
            pl.BlockSpec((HALO, D_MODEL), lambda i: (jnp.minimum((rev(i) + 1) * hb, n_halo - 1), u_col)),
            state_blk, state_blk,
            pl.BlockSpec((None, CHUNK, D_QK), lay3),
            pl.BlockSpec((None, CHUNK, D_QK), lay3),
            pl.BlockSpec((None, PAIRS, 2 * DK, CHUNK), lay4),
            pl.BlockSpec((None, PAIRS, 2 * DK, DV), lay4),
            pl.BlockSpec((None, PAIRS, CHUNK, 2 * CHUNK), lay4),
            pl.BlockSpec(band.shape, lambda i: (0, 0, 0, 0)),
            pl.BlockSpec((None, D_MODEL, D_MODEL), lay3),
        ],
        out_specs=pl.BlockSpec((tb, D_MODEL), lambda i: (rev_out(i), 0)),
        out_shape=jax.ShapeDtypeStruct((t, D_MODEL), F32),
        scratch_shapes=[
            pltpu.VMEM((tb, D_MODEL), F32),
            pltpu.VMEM((tb, D_MODEL), F32),
            pltpu.VMEM((tb - POOL_SUB + POOL_WIN, D_MODEL), BF16),
            pltpu.VMEM((2, tb, D_MODEL), BF16),
            pltpu.VMEM((PAIRS, 2 * DK, DV), F32),
        ],
        compiler_params=_params(),
    )(proj, proj, proj, kt, sf, xif, xib, zbt, dcb, m_tab, band, w["w_out"])


def _mlp_body(x_ref, z_ref, gmix_ref, gpre_ref, w1_ref, w2_ref, gpost_ref, out_ref, x1_rd, h_rd, x1_wr, h_wr):
    tb = x_ref.shape[0]
    n_ff = D_FF // FF_CHUNK
    rows_per = tb // (2 * n_ff)

    def norms(k):
        rows = slice(k * rows_per, (k + 1) * rows_per)
        x1 = x_ref[rows, :] + _rms(z_ref[rows, :], gmix_ref[...])
        x1_wr[rows, :] = x1
        h_wr[rows, :] = _rms(x1, gpre_ref[...]).astype(BF16)
        return _zero_like_row(x1[0:1, :])

    h = h_rd[...]
    acc = jnp.zeros((tb, D_MODEL), F32)
    for j in range(n_ff):
        lo = j * FF_CHUNK
        floor = jnp.maximum(norms(2 * j), norms(2 * j + 1))[:, 0:FF_CHUNK]
        f = jnp.dot(h, w1_ref[:, lo:lo + FF_CHUNK], preferred_element_type=F32)
        f = jnp.square(jnp.maximum(f, floor)).astype(BF16)
        acc = acc + jnp.dot(f, w2_ref[lo:lo + FF_CHUNK, :], preferred_element_type=F32)
    out_ref[...] = x1_rd[...] + _rms(acc, gpost_ref[...])


def _mlp_kernel(x_ref, z_ref, gmix_ref, gpre_ref, w1_ref, w2_ref, gpost_ref, out_ref, x1_a, h_a, x1_b, h_b):
    step = pl.program_id(0)
    args = (x_ref, z_ref, gmix_ref, gpre_ref, w1_ref, w2_ref, gpost_ref, out_ref)

    @pl.when(step == 0)
    def _():
        x1 = x_ref[...] + _rms(z_ref[...], gmix_ref[...])
        x1_a[...] = x1
        h_a[...] = _rms(x1, gpre_ref[...]).astype(BF16)

    @pl.when((lax.rem(step, 2) == 0) & (step > 0))
    def _():
        _mlp_body(*args, x1_b, h_b, x1_a, h_a)

    @pl.when(lax.rem(step, 2) == 1)
    def _():
        _mlp_body(*args, x1_a, h_a, x1_b, h_b)


def _mlp(x, z, layer, w):
    t = x.shape[0]
    tb = TB_MLP
    nb = t // tb
    lay3 = lambda i: (layer, 0, 0)
    row_l = pl.BlockSpec((None, 1, D_MODEL), lay3)
    cur = lambda i: (jnp.minimum(i, nb - 1), 0)
    return pl.pallas_call(
        _mlp_kernel,
        grid=(nb + 1,),
        in_specs=[
            pl.BlockSpec((tb, D_MODEL), cur),
            pl.BlockSpec((tb, D_MODEL), cur),
            row_l, row_l,
            pl.BlockSpec((None, D_MODEL, D_FF), lay3, pipeline_mode=pl.Buffered(1)),
            pl.BlockSpec((None, D_FF, D_MODEL), lay3, pipeline_mode=pl.Buffered(1)),
            row_l,
        ],
        out_specs=pl.BlockSpec((tb, D_MODEL), lambda i: (jnp.maximum(i - 1, 0), 0)),
        out_shape=jax.ShapeDtypeStruct((t, D_MODEL), F32),
        scratch_shapes=[pltpu.VMEM((tb, D_MODEL), F32), pltpu.VMEM((tb, D_MODEL), BF16),
                        pltpu.VMEM((tb, D_MODEL), F32), pltpu.VMEM((tb, D_MODEL), BF16)],
        compiler_params=_params(),
    )(x, z, w["g_mix_post"], w["g_mlp_pre"], w["w_mlp1"], w["w_mlp2"], w["g_mlp_post"])


def _qk_column_order():
    n = np.arange(D_QK)
    pair, l = n // 128, n % 128
    head = 2 * pair + (l // 32) % 2
    return head * DK + (l // 64) * (DK // 2) + l % 32


def _rope_tables(seq_len):
    half = DK // 2
    inv = ROPE_BASE ** (-jnp.arange(half, dtype=F32) / half)
    ang_i = (jnp.arange(seq_len // CHUNK, dtype=F32) * CHUNK)[:, None] * inv[None, :]
    ang_j = jnp.arange(CHUNK, dtype=F32)[:, None] * inv[None, :]
    ci, si = jnp.cos(ang_i)[:, None, :], jnp.sin(ang_i)[:, None, :]
    cj, sj = jnp.cos(ang_j)[None, :, :], jnp.sin(ang_j)[None, :, :]
    cos = (ci * cj - si * sj).reshape(seq_len, half)
    sin = (si * cj + ci * sj).reshape(seq_len, half)
    return jnp.tile(cos, (1, 4)), jnp.concatenate([-sin, -sin, sin, sin], axis=1)


def _trunk(x, w, tabs, cos_t, sin_t):
    batch, seq_len, d_model = x.shape
    assert d_model == D_MODEL and x.dtype == F32
    assert seq_len % TB_IN == 0 and seq_len % TB_MIX == 0 and (batch * seq_len) % TB_MLP == 0
    xt = x.reshape(batch * seq_len, D_MODEL)
    m_tab, xif, xib, zft, zbt, dcf, dcb = tabs
    for layer in range(DEPTH):
        proj, kt, sf = _inproj(xt, layer, w, (zft, dcf), cos_t, sin_t, seq_len)
        z = _mixer(proj, kt, sf, layer, w, (xif, xib, zbt, dcb, m_tab), seq_len)
        xt = _mlp(xt, z, layer, w)
    return xt.reshape(batch, seq_len, D_MODEL)


def kernel(x_prompt, x_sample, norm_mix_pre, norm_mix_post, w_in, ret_decay_fwd, ret_decay_bwd, ret_gn,
           pool_w, pool_scale, w_out, norm_mlp_pre, norm_mlp_post, w_mlp1, w_mlp2):
    w = dict(
        w_qk=_permute_qk(w_in), w_in=w_in.astype(BF16),
        w_u=_fold_pool(w_in, pool_w, pool_scale), w_out=w_out.astype(BF16),
        w_mlp1=w_mlp1.astype(BF16), w_mlp2=w_mlp2.astype(BF16),
        g_mix_pre=norm_mix_pre[:, None, :], g_mix_post=norm_mix_post[:, None, :], gn=ret_gn[:, None, :],
        g_mlp_pre=norm_mlp_pre[:, None, :], g_mlp_post=norm_mlp_post[:, None, :])
    dec = jnp.concatenate([ret_decay_fwd, ret_decay_bwd], axis=1).astype(F32)
    tabs = _decay_tables(dec)
    cos_t, sin_t = _rope_tables(max(x_prompt.shape[1], x_sample.shape[1]))
    y_prompt = _trunk(x_prompt, w, tabs, cos_t, sin_t)
    y_sample = _trunk(x_sample, w, tabs, cos_t, sin_t)
    return (y_prompt, y_sample)
```
